```python
import math
import jax, jax.numpy as jnp
from jax import lax
import numpy as np

D_MODEL = 2048
BATCH = 2
SEQ = 8192
DEPTH = 2

A_HEADS = 8
A_HEAD_DIM = 128
DILATED_GROUPS = ((128, 1), (512, 4), (2048, 16))
N_DIL = len(DILATED_GROUPS)
ATT_BLOCK = 128
ROPE_THETA = 500000.0
ROT_DIM = A_HEAD_DIM // 4
A_WIDTH = A_HEADS * A_HEAD_DIM
A_QKV = N_DIL * A_WIDTH
RET_HEADS = 8
RET_QK_DIM = 128
RET_V_DIM = 2 * RET_QK_DIM
RET_CHUNK = 128
RET_ROPE_THETA = 10000.0
RET_QK_WIDTH = RET_HEADS * RET_QK_DIM
RET_V_WIDTH = RET_HEADS * RET_V_DIM
EVEN_WIDTH = A_WIDTH + RET_V_WIDTH
EVEN_SPLITS = (A_QKV, A_QKV, A_QKV, RET_QK_WIDTH, RET_QK_WIDTH, RET_V_WIDTH, EVEN_WIDTH)
EVEN_IN = sum(EVEN_SPLITS)
CONV_WIDTH = 2 * D_MODEL
CONV_KERNEL = 31
EPS = 1e-6
NEG_INF = -1e30
N_EVEN = (DEPTH + 1) // 2
N_ODD = DEPTH // 2

kernel_name = "hybrid_dilated_retention_conformer"


def rms_norm(x, g):
    xf = x.astype(jnp.float32)
    y = xf * lax.rsqrt(jnp.mean(xf * xf, axis=-1, keepdims=True) + EPS)
    return (y * g.astype(jnp.float32)).astype(x.dtype)


def layer_norm(x, g, b):
    xf = x.astype(jnp.float32)
    mu = jnp.mean(xf, axis=-1, keepdims=True)
    var = jnp.mean(jnp.square(xf - mu), axis=-1, keepdims=True)
    y = (xf - mu) * lax.rsqrt(var + EPS)
    return (y * g.astype(jnp.float32) + b.astype(jnp.float32)).astype(x.dtype)


def rope(x, pos, rot_dim, theta):
    half = rot_dim // 2
    inv_freq = theta ** (-jnp.arange(0, rot_dim, 2, dtype=jnp.float32) / rot_dim)
    ang = pos[:, None] * inv_freq[None, :]
    shape = (ang.shape[0],) + (1,) * (x.ndim - 3) + (half,)
    cos = jnp.cos(ang).reshape(shape)
    sin = jnp.sin(ang).reshape(shape)
    xf = x.astype(jnp.float32)
    x1, x2, rest = xf[..., :half], xf[..., half:rot_dim], xf[..., rot_dim:]
    out = jnp.concatenate([x1 * cos - x2 * sin, x2 * cos + x1 * sin, rest], axis=-1)
    return out.astype(x.dtype)


def dilated_group_attention(q, k, v, window, dilation):
    B, S, H, dh = q.shape
    L = S // dilation
    nb = -(-L // ATT_BLOCK)
    Lp = nb * ATT_BLOCK
    w_sub = window // dilation

    def to_blocks(t):
        t = t.reshape(B, L, dilation, H, dh).transpose(0, 2, 3, 1, 4)
        t = jnp.pad(t, ((0, 0), (0, 0), (0, 0), (0, Lp - L), (0, 0)))
        return t.reshape(B, dilation, H, nb, ATT_BLOCK, dh)

    def with_prev(t):
        prev = jnp.concatenate([jnp.zeros_like(t[:, :, :, :1]), t[:, :, :, :-1]], axis=3)
        return jnp.concatenate([prev, t], axis=4)

    qb = to_blocks(q)
    kk = with_prev(to_blocks(k))
    vv = with_prev(to_blocks(v))
    s = jnp.einsum('bghnqc,bghnkc->bghnqk', qb, kk).astype(jnp.float32) * (dh ** -0.5)
    i = jnp.arange(ATT_BLOCK)[:, None]
    j = jnp.arange(2 * ATT_BLOCK)[None, :]
    dist = i + ATT_BLOCK - j
    band = (dist >= 0) & (dist <= w_sub)
    valid_prev = (jnp.arange(nb)[:, None, None] > 0) | (j >= ATT_BLOCK)[None]
    mask = band[None] & valid_prev
    s = jnp.where(mask, s, NEG_INF)
    lse = jax.nn.logsumexp(s, axis=-1)
    p = jnp.exp(s - lse[..., None])
    o = jnp.einsum('bghnqk,bghnkc->bghnqc', p.astype(v.dtype), vv)
    o = o.reshape(B, dilation, H, Lp, dh)[:, :, :, :L]
    o = o.transpose(0, 3, 1, 2, 4).reshape(B, S, H, dh)
    lse = lse.reshape(B, dilation, H, Lp)[:, :, :, :L]
    lse = lse.transpose(0, 3, 1, 2).reshape(B, S, H)
    return o, lse


def retention(q, k, v):
    B, S, H, dk = q.shape
    dv = v.shape[-1]
    C = RET_CHUNK
    N = S // C
    dt = q.dtype

    def to_chunks(t):
        return t.reshape(B, N, C, H, t.shape[-1]).transpose(0, 3, 1, 2, 4)

    qc, kc, vc = to_chunks(q), to_chunks(k), to_chunks(v)
    log_g = jnp.log1p(-jnp.power(2.0, -5.0 - jnp.arange(H, dtype=jnp.float32)))
    idx = jnp.arange(C, dtype=jnp.float32)
    diff = idx[:, None] - idx[None, :]
    dmask = jnp.where(diff >= 0, jnp.exp(jnp.maximum(diff, 0.0)[None] * log_g[:, None, None]), 0.0)
    scores = jnp.einsum('bhnid,bhnjd->bhnij', qc, kc) * dmask[None, :, None].astype(dt)
    intra = jnp.einsum('bhnij,bhnje->bhnie', scores, vc)
    k_dec = kc * jnp.exp((C - 1 - idx)[None, :] * log_g[:, None])[None, :, None, :, None].astype(dt)
    kv = jnp.einsum('bhnjd,bhnje->nbhde', k_dec, vc)
    chunk_decay = jnp.exp(C * log_g)[None, :, None, None].astype(dt)

    def step(state, kv_n):
        return state * chunk_decay + kv_n, state

    _, s_prev = lax.scan(step, jnp.zeros((B, H, dk, dv), dt), kv)
    q_dec = qc * jnp.exp((idx + 1.0)[None, :] * log_g[:, None])[None, :, None, :, None].astype(dt)
    cross = jnp.einsum('bhnid,nbhde->bhnie', q_dec, s_prev)
    out = (intra + cross).transpose(0, 2, 3, 1, 4).reshape(B, S, H, dv)
    return out


def head_rms(x):
    xf = x.astype(jnp.float32)
    return (xf * lax.rsqrt(jnp.mean(xf * xf, axis=-1, keepdims=True) + EPS)).astype(x.dtype)


def even_layer(x, norm_g, w_in, w_out, pos):
    B, S, _ = x.shape
    h = rms_norm(x, norm_g)
    proj = h @ w_in
    cuts = list(np.cumsum(EVEN_SPLITS)[:-1])
    aq, ak, av, bq, bk, bv, gate = jnp.split(proj, cuts, axis=-1)
    aq = rope(aq.reshape(B, S, N_DIL, A_HEADS, A_HEAD_DIM), pos, ROT_DIM, ROPE_THETA)
    ak = rope(ak.reshape(B, S, N_DIL, A_HEADS, A_HEAD_DIM), pos, ROT_DIM, ROPE_THETA)
    av = av.reshape(B, S, N_DIL, A_HEADS, A_HEAD_DIM)
    outs, lses = [], []
    for g, (window, dilation) in enumerate(DILATED_GROUPS):
        o, l = dilated_group_attention(aq[:, :, g], ak[:, :, g], av[:, :, g], window, dilation)
        outs.append(o)
        lses.append(l)
    wts = jax.nn.softmax(jnp.stack(lses, axis=0), axis=0)
    a_out = jnp.sum(wts[..., None].astype(x.dtype) * jnp.stack(outs, axis=0), axis=0)
    bq = rope(bq.reshape(B, S, RET_HEADS, RET_QK_DIM), pos, RET_QK_DIM, RET_ROPE_THETA)
    bk = rope(bk.reshape(B, S, RET_HEADS, RET_QK_DIM), pos, RET_QK_DIM, RET_ROPE_THETA) * (RET_QK_DIM ** -0.5)
    bv = bv.reshape(B, S, RET_HEADS, RET_V_DIM)
    b_out = head_rms(retention(bq, bk, bv))
    y = jnp.concatenate([a_out.reshape(B, S, A_WIDTH), b_out.reshape(B, S, RET_V_WIDTH)], axis=-1)
    y = y * jax.nn.silu(gate)
    return x + y @ w_out


def odd_layer(x, norm_g, w_in, b_in, conv_w, conv_b, ln_g, ln_b, w_out, b_out):
    h = rms_norm(x, norm_g)
    proj = h @ w_in + b_in
    a, b, gate = jnp.split(proj, 3, axis=-1)
    u = a * jax.nn.sigmoid(b)
    u = lax.conv_general_dilated(
        u, conv_w[:, None, :].astype(u.dtype), window_strides=(1,),
        padding=((CONV_KERNEL - 1, 0),), dimension_numbers=('NWC', 'WIO', 'NWC'),
        feature_group_count=CONV_WIDTH) + conv_b
    u = jax.nn.silu(layer_norm(u, ln_g, ln_b))
    y = u * jax.nn.silu(gate)
    return x + y @ w_out + b_out


def setup_inputs(seed: int = 0) -> dict:
    key = jax.random.key(seed)
    ks = jax.random.split(key, 16)
    f32 = jnp.float32
    nrm = lambda k, shape, scale: jax.random.normal(k, shape, f32) * scale
    return {
        "x": nrm(ks[0], (BATCH, SEQ, D_MODEL), 1.0),
        "norm_even": 1.0 + nrm(ks[1], (N_EVEN, D_MODEL), 0.02),
        "w_in_even": nrm(ks[2], (N_EVEN, D_MODEL, EVEN_IN), D_MODEL ** -0.5),
        "w_out_even": nrm(ks[3], (N_EVEN, EVEN_WIDTH, D_MODEL), EVEN_WIDTH ** -0.5),
        "norm_odd": 1.0 + nrm(ks[4], (N_ODD, D_MODEL), 0.02),
        "w_in_odd": nrm(ks[5], (N_ODD, D_MODEL, 3 * CONV_WIDTH), D_MODEL ** -0.5),
        "b_in_odd": nrm(ks[6], (N_ODD, 3 * CONV_WIDTH), 0.02),
        "conv_w_odd": nrm(ks[7], (N_ODD, CONV_KERNEL, CONV_WIDTH), CONV_KERNEL ** -0.5),
        "conv_b_odd": nrm(ks[8], (N_ODD, CONV_WIDTH), 0.02),
        "ln_g_odd": 1.0 + nrm(ks[9], (N_ODD, CONV_WIDTH), 0.02),
        "ln_b_odd": nrm(ks[10], (N_ODD, CONV_WIDTH), 0.02),
        "w_out_odd": nrm(ks[11], (N_ODD, CONV_WIDTH, D_MODEL), CONV_WIDTH ** -0.5),
        "b_out_odd": nrm(ks[12], (N_ODD, D_MODEL), 0.02),
        "final_norm": 1.0 + nrm(ks[13], (D_MODEL,), 0.02),
    }


def reference(x, norm_even, w_in_even, w_out_even, norm_odd, w_in_odd, b_in_odd,
              conv_w_odd, conv_b_odd, ln_g_odd, ln_b_odd, w_out_odd, b_out_odd, final_norm):
    pos = jnp.arange(x.shape[1], dtype=jnp.float32)
    for layer in range(DEPTH):
        i = layer // 2
        if layer % 2 == 0:
            x = even_layer(x, norm_even[i], w_in_even[i], w_out_even[i], pos)
        else:
            x = odd_layer(x, norm_odd[i], w_in_odd[i], b_in_odd[i], conv_w_odd[i], conv_b_odd[i],
                          ln_g_odd[i], ln_b_odd[i], w_out_odd[i], b_out_odd[i])
    return rms_norm(x, final_norm)
```

```python
import functools

import numpy as np
import jax
import jax.numpy as jnp
from jax import lax
from jax.experimental import pallas as pl
from jax.experimental.pallas import tpu as pltpu

F32 = jnp.float32
BF16 = jnp.bfloat16

D_MODEL = 2048
A_HEADS = 8
HEAD_DIM = 128
DILATIONS = (1, 4, 16)
ATT_BLOCK = 128
ROPE_THETA = 500000.0
ROT_DIM = HEAD_DIM // 4
A_WIDTH = A_HEADS * HEAD_DIM
RET_HEADS = 8
RET_QK_DIM = 128
RET_V_DIM = 256
RET_CHUNK = 128
RET_ROPE_THETA = 10000.0
EVEN_IN = 16384
EVEN_WIDTH = 3072
CONV_WIDTH = 4096
CONV_KERNEL = 31
EPS = 1e-6
NEG_INF = -1e30

NCLS = 16
CHUNK = NCLS * ATT_BLOCK
HALO = 32

VMEM_LIMIT_V7X = 56 * 1024 * 1024


def _params(sem):
    return pltpu.CompilerParams(dimension_semantics=sem, vmem_limit_bytes=VMEM_LIMIT_V7X)


def _dot(a, b):
    return jnp.dot(a, b, preferred_element_type=F32)


def _dot_nt(a, b):
    return lax.dot_general(a, b, (((1,), (1,)), ((), ())), preferred_element_type=F32)


def _dot_tn(a, b):
    return lax.dot_general(a, b, (((0,), (0,)), ((), ())), preferred_element_type=F32)


def _sigmoid(x):
    return 1.0 / (1.0 + jnp.exp(-x))


def _rms(x, g):
    ms = jnp.mean(x * x, axis=-1, keepdims=True)
    return x * lax.rsqrt(ms + EPS) * g


def _rmsnorm_kernel(x_ref, g_ref, o_ref):
    o_ref[...] = _rms(x_ref[...], g_ref[...]).astype(o_ref.dtype)


def _rmsnorm(x, g, bm=512):
    t, d = x.shape
    return pl.pallas_call(
        _rmsnorm_kernel,
        grid=(t // bm,),
        in_specs=[pl.BlockSpec((bm, d), lambda i: (i, 0)),
                  pl.BlockSpec((1, d), lambda i: (0, 0))],
        out_specs=pl.BlockSpec((bm, d), lambda i: (i, 0)),
        out_shape=jax.ShapeDtypeStruct((t, d), BF16),
        compiler_params=_params(("parallel",)),
        name="rmsnorm_even",
    )(x, g)


_ROPE_ROWS = 256


def _even_inproj_kernel(h_ref, w_ref, tab_ref, o_ref, acc_ref, *, bm):
    j = pl.program_id(1)
    acc_ref[...] = _dot(h_ref[...], w_ref[...])
    is_a = j < 6
    is_b = jnp.logical_and(j >= 9, j < 11)

    @pl.when(is_a)
    def _():
        base = jnp.where(j < 3, 0, 3)

        def body(r, carry):
            rows = pl.ds(pl.multiple_of(r * _ROPE_ROWS, _ROPE_ROWS), _ROPE_ROWS)
            c = tab_ref[base, rows, :]
            sa = tab_ref[base + 1, rows, :]
            sb = tab_ref[base + 2, rows, :]
            for hh in range(A_HEADS):
                cols = slice(hh * HEAD_DIM, (hh + 1) * HEAD_DIM)
                x = acc_ref[rows, cols]
                y = x * c + pltpu.roll(x, ROT_DIM // 2, 1) * sa \
                    + pltpu.roll(x, HEAD_DIM - ROT_DIM // 2, 1) * sb
                o_ref[rows, cols] = y.astype(o_ref.dtype)
            return carry

        lax.fori_loop(0, bm // _ROPE_ROWS, body, 0)

    @pl.when(is_b)
    def _():
        base = jnp.where(j == 9, 6, 8)

        def body(r, carry):
            rows = pl.ds(pl.multiple_of(r * _ROPE_ROWS, _ROPE_ROWS), _ROPE_ROWS)
            c = tab_ref[base, rows, :]
            s = tab_ref[base + 1, rows, :]
            for hh in range(RET_HEADS):
                cols = slice(hh * RET_QK_DIM, (hh + 1) * RET_QK_DIM)
                x = acc_ref[rows, cols]
                y = x * c + pltpu.roll(x, RET_QK_DIM // 2, 1) * s
                o_ref[rows, cols] = y.astype(o_ref.dtype)
            return carry

        lax.fori_loop(0, bm // _ROPE_ROWS, body, 0)

    @pl.when(jnp.logical_not(jnp.logical_or(is_a, is_b)))
    def _():
        o_ref[...] = acc_ref[...].astype(o_ref.dtype)


def _even_inproj(h, w, tabs, seq, bm=1024, bn=1024):
    t, d = h.shape
    n = w.shape[1]
    sblocks = seq // bm
    return pl.pallas_call(
        functools.partial(_even_inproj_kernel, bm=bm),
        grid=(t // bm, n // bn),
        in_specs=[pl.BlockSpec((bm, d), lambda i, j: (i, 0)),
                  pl.BlockSpec((d, bn), lambda i, j: (0, j)),
                  pl.BlockSpec((tabs.shape[0], bm, HEAD_DIM), lambda i, j: (0, i % sblocks, 0))],
        out_specs=pl.BlockSpec((bm, bn), lambda i, j: (i, j)),
        out_shape=jax.ShapeDtypeStruct((t, n), BF16),
        scratch_shapes=[pltpu.VMEM((bm, bn), F32)],
        compiler_params=_params(("parallel", "arbitrary")),
        name="even_inproj",
    )(h, w, tabs)


def _rope_tables(seq):
    pos = jnp.arange(seq, dtype=F32)
    half = ROT_DIM // 2
    inv = ROPE_THETA ** (-jnp.arange(0, ROT_DIM, 2, dtype=F32) / ROT_DIM)
    ang = pos[:, None] * inv[None, :]
    cos, sin = jnp.cos(ang), jnp.sin(ang)
    zeros = jnp.zeros((seq, HEAD_DIM - ROT_DIM), F32)
    zh = jnp.zeros((seq, half), F32)
    c_a = jnp.concatenate([cos, cos, jnp.ones_like(zeros)], axis=-1)
    sa_a = jnp.concatenate([zh, sin, zeros], axis=-1)
    sb_a = jnp.concatenate([-sin, zh, zeros], axis=-1)
    qs = HEAD_DIM ** -0.5
    invb = RET_ROPE_THETA ** (-jnp.arange(0, RET_QK_DIM, 2, dtype=F32) / RET_QK_DIM)
    angb = pos[:, None] * invb[None, :]
    cosb, sinb = jnp.cos(angb), jnp.sin(angb)
    c_b = jnp.concatenate([cosb, cosb], axis=-1)
    s_b = jnp.concatenate([-sinb, sinb], axis=-1)
    ks = RET_QK_DIM ** -0.5
    return jnp.stack([c_a * qs, sa_a * qs, sb_a * qs, c_a, sa_a, sb_a,
                      c_b, s_b, c_b * ks, s_b * ks], axis=0)


def _attention_masks():
    def table(pos_q, pos_k, first_from):
        dist = pos_q[:, None] - pos_k[None, :]
        ok = (dist >= 0) & (dist <= ATT_BLOCK)
        first = ok & (pos_k[None, :] >= first_from)
        return np.where(np.stack([ok, first]), 0.0, NEG_INF).astype(np.float32)

    i = np.arange(ATT_BLOCK)
    m16 = table(ATT_BLOCK + i, np.arange(2 * ATT_BLOCK), ATT_BLOCK)
    a, j = np.divmod(np.arange(ATT_BLOCK), 32)
    ak, jk = np.divmod(np.arange(2 * ATT_BLOCK), 64)
    m4 = table(ATT_BLOCK + 4 * j + a, 4 * jk + ak, ATT_BLOCK)
    r, j = np.divmod(np.arange(2 * ATT_BLOCK), 16)
    rk, jk = np.divmod(np.arange(4 * ATT_BLOCK), 32)
    m1 = table(2 * ATT_BLOCK + 16 * j + r, 16 * jk + rk, 2 * ATT_BLOCK)
    return jnp.asarray(m1), jnp.asarray(m4), jnp.asarray(m16)


def _online_block(q, k, v, bias, m_old, l_old, acc_old):
    nq, nk = q.shape[0], k.shape[0]
    s = _dot_nt(q, k) + bias
    m_blk = jnp.max(s, axis=1, keepdims=True)
    if m_old is None:
        m_new = jnp.broadcast_to(m_blk, (nq, HEAD_DIM))
    else:
        m_new = jnp.maximum(m_old, m_blk)
    p = jnp.exp(s - jnp.concatenate([m_new] * (nk // HEAD_DIM), axis=1))
    l_blk = jnp.sum(p, axis=1, keepdims=True)
    o = _dot(p.astype(BF16), v)
    if m_old is None:
        return m_new, jnp.broadcast_to(l_blk, (nq, HEAD_DIM)), o
    alpha = jnp.exp(m_old - m_new)
    return m_new, alpha * l_old + l_blk, alpha * acc_old + o


def _attention_kernel(q1_ref, q4_ref, q16_ref, kc1_ref, kc4_ref, kc16_ref,
                      vc1_ref, vc4_ref, vc16_ref, kp1_ref, kp4_ref, kp16_ref,
                      vp1_ref, vp4_ref, vp16_ref, m1_ref, m4_ref, m16_ref,
                      o_ref, k4_ref, v4_ref, k1_ref, v1_ref, acc_ref, mst_ref, lst_ref):
    c = pl.program_id(1)
    first_chunk = c == 0

    def body16(r, carry):
        k = jnp.concatenate([kp16_ref[r], kc16_ref[r]], axis=0)
        v = jnp.concatenate([vp16_ref[r], vc16_ref[r]], axis=0)
        bias = m16_ref[first_chunk.astype(jnp.int32)]
        m, l, o = _online_block(q16_ref[r], k, v, bias, None, None, None)
        mst_ref[r] = m
        lst_ref[r] = l
        acc_ref[r] = o
        return carry

    lax.fori_loop(0, NCLS, body16, 0)

    k4_ref[:, 0:32, :] = kp4_ref[...]
    k4_ref[:, 32:32 + ATT_BLOCK, :] = kc4_ref[...]
    v4_ref[:, 0:32, :] = vp4_ref[...]
    v4_ref[:, 32:32 + ATT_BLOCK, :] = vc4_ref[...]

    def body4(idx, carry):
        r4 = idx % 4
        nn = idx // 4
        qrows = pl.ds(pl.multiple_of(nn * 32, 32), 32)
        krows = pl.ds(pl.multiple_of(nn * 32, 32), 64)
        cls = [r4 + 4 * a for a in range(4)]
        q = jnp.concatenate([q4_ref[cl, qrows, :] for cl in cls], axis=0)
        k = jnp.concatenate([k4_ref[cl, krows, :] for cl in cls], axis=0)
        v = jnp.concatenate([v4_ref[cl, krows, :] for cl in cls], axis=0)
        first = jnp.logical_and(first_chunk, nn == 0).astype(jnp.int32)
        m_old = jnp.concatenate([mst_ref[cl, qrows, :] for cl in cls], axis=0)
        l_old = jnp.concatenate([lst_ref[cl, qrows, :] for cl in cls], axis=0)
        a_old = jnp.concatenate([acc_ref[cl, qrows, :] for cl in cls], axis=0)
        m, l, o = _online_block(q, k, v, m4_ref[first], m_old, l_old, a_old)
        for a, cl in enumerate(cls):
            part = slice(32 * a, 32 * (a + 1))
            mst_ref[cl, qrows, :] = m[part]
            lst_ref[cl, qrows, :] = l[part]
            acc_ref[cl, qrows, :] = o[part]
        return carry

    lax.fori_loop(0, 16, body4, 0)

    k1_ref[:, 0:16, :] = kp1_ref[...]
    k1_ref[:, 16:16 + ATT_BLOCK, :] = kc1_ref[...]
    v1_ref[:, 0:16, :] = vp1_ref[...]
    v1_ref[:, 16:16 + ATT_BLOCK, :] = vc1_ref[...]

    def body1(mb, carry):
        qrows = pl.ds(pl.multiple_of(mb * 16, 16), 16)
        krows = pl.ds(pl.multiple_of(mb * 16, 16), 32)
        q = jnp.concatenate([q1_ref[cl, qrows, :] for cl in range(NCLS)], axis=0)
        k = jnp.concatenate([k1_ref[cl, krows, :] for cl in range(NCLS)], axis=0)
        v = jnp.concatenate([v1_ref[cl, krows, :] for cl in range(NCLS)], axis=0)
        first = jnp.logical_and(first_chunk, mb == 0).astype(jnp.int32)
        m_old = jnp.concatenate([mst_ref[cl, qrows, :] for cl in range(NCLS)], axis=0)
        l_old = jnp.concatenate([lst_ref[cl, qrows, :] for cl in range(NCLS)], axis=0)
        a_old = jnp.concatenate([acc_ref[cl, qrows, :] for cl in range(NCLS)], axis=0)
        _, l, o = _online_block(q, k, v, m1_ref[first], m_old, l_old, a_old)
        res = o / l
        for cl in range(NCLS):
            o_ref[cl, qrows, :] = res[16 * cl:16 * (cl + 1)].astype(o_ref.dtype)
        return carry

    lax.fori_loop(0, 8, body1, 0)


def _attention(pa, masks, batch, seq):
    lcls = seq // NCLS
    nchunk = seq // CHUNK
    nh = A_HEADS
    m1, m4, m16 = masks

    def cur(colbase):
        return pl.BlockSpec((None, NCLS, ATT_BLOCK, HEAD_DIM),
                            lambda b, c, h: (b, 0, c, colbase + h))

    def prev(colbase, rows):
        per = ATT_BLOCK // rows
        return pl.BlockSpec((None, NCLS, rows, HEAD_DIM),
                            lambda b, c, h: (b, 0, jnp.maximum(c * per - 1, 0), colbase + h))

    def const(m):
        return pl.BlockSpec(m.shape, lambda b, c, h: (0, 0, 0))

    qb, kb, vb = 0, 3 * nh, 6 * nh
    in_specs = [cur(qb), cur(qb + nh), cur(qb + 2 * nh),
                cur(kb), cur(kb + nh), cur(kb + 2 * nh),
                cur(vb), cur(vb + nh), cur(vb + 2 * nh),
                prev(kb, 16), prev(kb + nh, 32), prev(kb + 2 * nh, ATT_BLOCK),
                prev(vb, 16), prev(vb + nh, 32), prev(vb + 2 * nh, ATT_BLOCK),
                const(m1), const(m4), const(m16)]
    return pl.pallas_call(
        _attention_kernel,
        grid=(batch, nchunk, nh),
        in_specs=in_specs,
        out_specs=pl.BlockSpec((None, NCLS, ATT_BLOCK, HEAD_DIM), lambda b, c, h: (b, 0, c, h)),
        out_shape=jax.ShapeDtypeStruct((batch, NCLS, lcls, A_WIDTH), BF16),
        scratch_shapes=[pltpu.VMEM((NCLS, 32 + ATT_BLOCK, HEAD_DIM), BF16),
                        pltpu.VMEM((NCLS, 32 + ATT_BLOCK, HEAD_DIM), BF16),
                        pltpu.VMEM((NCLS, 16 + ATT_BLOCK, HEAD_DIM), BF16),
                        pltpu.VMEM((NCLS, 16 + ATT_BLOCK, HEAD_DIM), BF16),
                        pltpu.VMEM((NCLS, ATT_BLOCK, HEAD_DIM), F32),
                        pltpu.VMEM((NCLS, ATT_BLOCK, HEAD_DIM), F32),
                        pltpu.VMEM((NCLS, ATT_BLOCK, HEAD_DIM), F32)],
        compiler_params=_params(("parallel", "parallel", "parallel")),
        name="dilated_attention",
    )(*([pa] * 15), m1, m4, m16)


def _retention_kernel(cd_ref, q_ref, k_ref, v0_ref, v1_ref, dm_ref, kd_ref, qd_ref,
                      o_ref, st_ref):
    @pl.when(pl.program_id(1) == 0)
    def _():
        st_ref[...] = jnp.zeros_like(st_ref)

    for h in range(RET_HEADS):
        cols = slice(h * RET_QK_DIM, (h + 1) * RET_QK_DIM)
        q = q_ref[:, cols]
        k = k_ref[:, cols]
        v_ref = v0_ref if h < RET_HEADS // 2 else v1_ref
        hv = h % (RET_HEADS // 2)
        v = v_ref[:, hv * RET_V_DIM:(hv + 1) * RET_V_DIM]
        state = st_ref[h]
        scores = _dot_nt(q, k) * dm_ref[h]
        intra = _dot(scores.astype(BF16), v)
        q_dec = (q.astype(F32) * qd_ref[h]).astype(BF16)
        cross = _dot(q_dec, state.astype(BF16))
        out = intra + cross
        ms = jnp.mean(out * out, axis=-1, keepdims=True)
        o_ref[:, h * RET_V_DIM:(h + 1) * RET_V_DIM] = (out * lax.rsqrt(ms + EPS)).astype(o_ref.dtype)
        k_dec = (k.astype(F32) * kd_ref[h]).astype(BF16)
        st_ref[h] = state * cd_ref[h] + _dot_tn(k_dec, v)


def _retention_tables():
    c = RET_CHUNK
    log_g = jnp.log1p(-jnp.power(2.0, -5.0 - jnp.arange(RET_HEADS, dtype=F32)))
    idx = jnp.arange(c, dtype=F32)
    diff = idx[:, None] - idx[None, :]
    dmask = jnp.where(diff >= 0, jnp.exp(jnp.maximum(diff, 0.0)[None] * log_g[:, None, None]), 0.0)
    kd = jnp.exp((c - 1 - idx)[None, :] * log_g[:, None])
    qd = jnp.exp((idx + 1.0)[None, :] * log_g[:, None])
    cd = jnp.exp(c * log_g)
    rep = lambda t: jnp.broadcast_to(t[:, :, None], (RET_HEADS, c, RET_QK_DIM))
    return cd, dmask, rep(kd), rep(qd)


def _retention(proj, tables, batch, seq):
    cd, dmask, kd, qd = tables
    t = proj.shape[0]
    nchunk = seq // RET_CHUNK
    qcol = 3 * 3 * A_WIDTH // (RET_HEADS * RET_QK_DIM)
    vcol = qcol + 2
    w = RET_HEADS * RET_QK_DIM

    def rows(col):
        return pl.BlockSpec((RET_CHUNK, w), lambda b, n: (b * nchunk + n, col))

    def const(a):
        return pl.BlockSpec(a.shape, lambda b, n: (0, 0, 0))

    return pl.pallas_call(
        _retention_kernel,
        grid=(batch, nchunk),
        in_specs=[pl.BlockSpec(memory_space=pltpu.SMEM),
                  rows(qcol), rows(qcol + 1), rows(vcol), rows(vcol + 1),
                  const(dmask), const(kd), const(qd)],
        out_specs=pl.BlockSpec((RET_CHUNK, RET_HEADS * RET_V_DIM), lambda b, n: (b * nchunk + n, 0)),
        out_shape=jax.ShapeDtypeStruct((t, RET_HEADS * RET_V_DIM), BF16),
        scratch_shapes=[pltpu.VMEM((RET_HEADS, RET_QK_DIM, RET_V_DIM), F32)],
        compiler_params=_params(("parallel", "arbitrary")),
        name="retention",
    )(cd, proj, proj, proj, proj, dmask, kd, qd)


def _even_out_kernel(a_ref, r_ref, g0_ref, g1_ref, g2_ref, x_ref, w_ref, ng_ref,
                     x1_ref, h2_ref):
    def gated(y, g_ref):
        g = g_ref[...].astype(F32)
        return (y.astype(F32) * (g * _sigmoid(g))).astype(BF16)

    acc = _dot(gated(a_ref[...], g0_ref), w_ref[0:1024, :])
    acc += _dot(gated(r_ref[:, 0:1024], g1_ref), w_ref[1024:2048, :])
    acc += _dot(gated(r_ref[:, 1024:2048], g2_ref), w_ref[2048:3072, :])
    x1 = x_ref[...] + acc
    x1_ref[...] = x1
    h2_ref[...] = _rms(x1, ng_ref[...]).astype(h2_ref.dtype)


def _even_out(a, r, proj, x, w, ng, bm=256):
    t = x.shape[0]
    gcol = 13
    row = lambda width: pl.BlockSpec((bm, width), lambda i: (i, 0))
    gate = lambda col: pl.BlockSpec((bm, 1024), lambda i: (i, col))
    return pl.pallas_call(
        _even_out_kernel,
        grid=(t // bm,),
        in_specs=[row(A_WIDTH), row(RET_HEADS * RET_V_DIM),
                  gate(gcol), gate(gcol + 1), gate(gcol + 2),
                  row(D_MODEL),
                  pl.BlockSpec(w.shape, lambda i: (0, 0), pipeline_mode=pl.Buffered(1)),
                  pl.BlockSpec((1, D_MODEL), lambda i: (0, 0))],
        out_specs=[row(D_MODEL), row(D_MODEL)],
        out_shape=[jax.ShapeDtypeStruct((t, D_MODEL), F32),
                   jax.ShapeDtypeStruct((t, D_MODEL), BF16)],
        compiler_params=_params(("parallel",)),
        name="even_outproj",
    )(a, r, proj, proj, proj, x, w, ng)


def _odd_inproj_kernel(h_ref, wa_ref, wb_ref, wg_ref, ba_ref, bb_ref, bg_ref, u_ref, sg_ref):
    h = h_ref[...]
    a = _dot(h, wa_ref[...]) + ba_ref[...]
    b = _dot(h, wb_ref[...]) + bb_ref[...]
    u_ref[...] = (a * _sigmoid(b)).astype(u_ref.dtype)
    g = _dot(h, wg_ref[...]) + bg_ref[...]
    sg_ref[...] = (g * _sigmoid(g)).astype(sg_ref.dtype)


def _odd_inproj(h, w, b, bm=1024, bn=512):
    t, d = h.shape
    nb = CONV_WIDTH // bn
    wspec = lambda k: pl.BlockSpec((d, bn), lambda i, j: (0, j + k * nb))
    bspec = lambda k: pl.BlockSpec((1, bn), lambda i, j: (0, j + k * nb))
    ospec = pl.BlockSpec((bm, bn), lambda i, j: (i, j))
    return pl.pallas_call(
        _odd_inproj_kernel,
        grid=(t // bm, nb),
        in_specs=[pl.BlockSpec((bm, d), lambda i, j: (i, 0)),
                  wspec(0), wspec(1), wspec(2), bspec(0), bspec(1), bspec(2)],
        out_specs=[ospec, ospec],
        out_shape=[jax.ShapeDtypeStruct((t, CONV_WIDTH), BF16)] * 2,
        compiler_params=_params(("parallel", "arbitrary")),
        name="odd_inproj",
    )(h, w, w, w, b, b, b)


_CONV_ROWS = 64


def _conv_kernel(u_ref, w_ref, b_ref, o_ref, ext_ref, *, bt):
    @pl.when(pl.program_id(2) == 0)
    def _():
        ext_ref[0:HALO, :] = jnp.zeros((HALO, ext_ref.shape[1]), F32)

    ext_ref[HALO:HALO + bt, :] = u_ref[...].astype(F32)
    lead = HALO - (CONV_KERNEL - 1)
    for rc in range(bt // _CONV_ROWS):
        r0 = rc * _CONV_ROWS
        acc = jnp.broadcast_to(b_ref[...], (_CONV_ROWS, ext_ref.shape[1]))
        for k in range(CONV_KERNEL):
            acc = acc + w_ref[k:k + 1, :] * ext_ref[r0 + lead + k:r0 + lead + k + _CONV_ROWS, :]
        o_ref[r0:r0 + _CONV_ROWS, :] = acc.astype(o_ref.dtype)
    ext_ref[0:HALO, :] = ext_ref[bt:bt + HALO, :]


def _conv(u, w, b, batch, seq, bt=512, bc=512):
    t = u.shape[0]
    nt = seq // bt
    return pl.pallas_call(
        functools.partial(_conv_kernel, bt=bt),
        grid=(batch, CONV_WIDTH // bc, nt),
        in_specs=[pl.BlockSpec((bt, bc), lambda bb, c, i: (bb * nt + i, c)),
                  pl.BlockSpec((HALO, bc), lambda bb, c, i: (0, c)),
                  pl.BlockSpec((1, bc), lambda bb, c, i: (0, c))],
        out_specs=pl.BlockSpec((bt, bc), lambda bb, c, i: (bb * nt + i, c)),
        out_shape=jax.ShapeDtypeStruct((t, CONV_WIDTH), BF16),
        scratch_shapes=[pltpu.VMEM((bt + HALO, bc), F32)],
        compiler_params=_params(("parallel", "parallel", "arbitrary")),
        name="causal_conv",
    )(u, w, b)


def _odd_out_kernel(c_ref, sg_ref, lg_ref, lb_ref, w_ref, x_ref, bo_ref, fg_ref, o_ref):
    c = c_ref[...].astype(F32)
    mu = jnp.mean(c, axis=-1, keepdims=True)
    cen = c - mu
    var = jnp.mean(cen * cen, axis=-1, keepdims=True)
    y = cen * lax.rsqrt(var + EPS) * lg_ref[...] + lb_ref[...]
    z = (y * _sigmoid(y)) * sg_ref[...].astype(F32)
    x2 = x_ref[...] + _dot(z.astype(BF16), w_ref[...]) + bo_ref[...]
    o_ref[...] = _rms(x2, fg_ref[...])


def _odd_out(c, sg, lg, lb, w, x, bo, fg, bm=256):
    t = x.shape[0]
    row = lambda width: pl.BlockSpec((bm, width), lambda i: (i, 0))
    vec = lambda width: pl.BlockSpec((1, width), lambda i: (0, 0))
    return pl.pallas_call(
        _odd_out_kernel,
        grid=(t // bm,),
        in_specs=[row(CONV_WIDTH), row(CONV_WIDTH), vec(CONV_WIDTH), vec(CONV_WIDTH),
                  pl.BlockSpec(w.shape, lambda i: (0, 0), pipeline_mode=pl.Buffered(1)),
                  row(D_MODEL), vec(D_MODEL), vec(D_MODEL)],
        out_specs=row(D_MODEL),
        out_shape=jax.ShapeDtypeStruct((t, D_MODEL), F32),
        compiler_params=_params(("parallel",)),
        name="odd_outproj",
    )(c, sg, lg, lb, w, x, bo, fg)


def _trunk(x, norm_even, w_in_even, w_out_even, norm_odd, w_in_odd, b_in_odd, conv_w_odd,
           conv_b_odd, ln_g_odd, ln_b_odd, w_out_odd, b_out_odd, final_norm):
    batch, seq, d = x.shape
    t = batch * seq
    row = lambda v: v.reshape(1, -1)
    x2d = x.reshape(t, d)

    h = _rmsnorm(x2d, row(norm_even[0]))
    proj = _even_inproj(h, w_in_even[0].astype(BF16), _rope_tables(seq), seq)
    a_cols = 9 * A_WIDTH
    pa = proj[:, :a_cols].reshape(batch, seq // NCLS, NCLS, a_cols).transpose(0, 2, 1, 3)
    a16 = _attention(pa, _attention_masks(), batch, seq)
    a = a16.transpose(0, 2, 1, 3).reshape(t, A_WIDTH)
    r = _retention(proj, _retention_tables(), batch, seq)
    x1, h2 = _even_out(a, r, proj, x2d, w_out_even[0].astype(BF16), row(norm_odd[0]))

    u, sg = _odd_inproj(h2, w_in_odd[0].astype(BF16), row(b_in_odd[0]))
    cw = jnp.concatenate([conv_w_odd[0], jnp.zeros((HALO - CONV_KERNEL, CONV_WIDTH), F32)], axis=0)
    c = _conv(u, cw, row(conv_b_odd[0]), batch, seq)
    out = _odd_out(c, sg, row(ln_g_odd[0]), row(ln_b_odd[0]), w_out_odd[0].astype(BF16),
                   x1, row(b_out_odd[0]), row(final_norm))
    return out.reshape(batch, seq, d)


def kernel(x, norm_even, w_in_even, w_out_even, norm_odd, w_in_odd, b_in_odd, conv_w_odd,
           conv_b_odd, ln_g_odd, ln_b_odd, w_out_odd, b_out_odd, final_norm):
    return _trunk(x, norm_even, w_in_even, w_out_even, norm_odd, w_in_odd, b_in_odd,
                  conv_w_odd, conv_b_odd, ln_g_odd, ln_b_odd, w_out_odd, b_out_odd, final_norm)
```

```python
import functools

import numpy as np
import jax
import jax.numpy as jnp
from jax import lax
from jax.experimental import pallas as pl
from jax.experimental.pallas import tpu as pltpu

F32 = jnp.float32
BF16 = jnp.bfloat16

D_MODEL = 2048
A_HEADS = 8
HEAD_DIM = 128
ATT_BLOCK = 128
ROPE_THETA = 500000.0
ROT_DIM = HEAD_DIM // 4
A_WIDTH = A_HEADS * HEAD_DIM
RET_HEADS = 8
RET_QK_DIM = 128
RET_V_DIM = 256
RET_CHUNK = 128
RET_ROPE_THETA = 10000.0
CONV_WIDTH = 4096
CONV_KERNEL = 31
EPS = 1e-6
NEG_INF = -1e30

NCLS = 16
CHUNK = NCLS * ATT_BLOCK
HALO = 32
LANES = 128
SUBLANES = 8

VMEM_LIMIT_V7X = 56 * 1024 * 1024


def _params(sem):
    return pltpu.CompilerParams(dimension_semantics=sem, vmem_limit_bytes=VMEM_LIMIT_V7X)


def _dot(a, b):
    return jnp.dot(a, b, preferred_element_type=F32)


def _dot_nt(a, b):
    return lax.dot_general(a, b, (((1,), (1,)), ((), ())), preferred_element_type=F32)


def _dot_tn(a, b):
    return lax.dot_general(a, b, (((0,), (0,)), ((), ())), preferred_element_type=F32)


def _sigmoid(x):
    return 1.0 / (1.0 + jnp.exp(-x))


def _rms(x, g):
    ms = jnp.mean(x * x, axis=-1, keepdims=True)
    return x * lax.rsqrt(ms + EPS) * g


def _rmsnorm_kernel(x_ref, g_ref, p_ref, h_ref, hc_ref, *, bl):
    h = _rms(x_ref[...], g_ref[...]).astype(h_ref.dtype)
    h_ref[...] = h
    hc = _dot(p_ref[...], h).astype(hc_ref.dtype)
    for r in range(NCLS):
        hc_ref[r] = hc[r * bl:(r + 1) * bl]


def _class_major_perm(bl):
    n = bl * NCLS
    out_row = np.arange(n)
    src = NCLS * (out_row % bl) + out_row // bl
    return jnp.asarray((src[:, None] == np.arange(n)[None, :]).astype(np.float32), dtype=BF16)


def _rmsnorm(x, g, batch, seq, bl=32):
    t, d = x.shape
    lcls = seq // NCLS
    nb = lcls // bl
    perm = _class_major_perm(bl)
    return pl.pallas_call(
        functools.partial(_rmsnorm_kernel, bl=bl),
        grid=(batch, nb),
        in_specs=[pl.BlockSpec((bl * NCLS, d), lambda b, i: (b * nb + i, 0)),
                  pl.BlockSpec((1, d), lambda b, i: (0, 0)),
                  pl.BlockSpec(perm.shape, lambda b, i: (0, 0))],
        out_specs=[pl.BlockSpec((bl * NCLS, d), lambda b, i: (b * nb + i, 0)),
                   pl.BlockSpec((None, NCLS, bl, d), lambda b, i: (b, 0, i, 0))],
        out_shape=[jax.ShapeDtypeStruct((t, d), BF16),
                   jax.ShapeDtypeStruct((batch, NCLS, lcls, d), BF16)],
        compiler_params=_params(("parallel", "parallel")),
        name="rmsnorm_even",
    )(x, g, perm)


_SUB = 256


def _inproj_kernel(h_ref, w_ref, *rest, mode, bn, q_blocks):
    if mode == "plain":
        (o_ref,) = rest
    else:
        tab_ref, o_ref = rest
        ntab = 3 if mode == "rope_a" else 2
        base = jnp.where(pl.program_id(1) < q_blocks, 0, ntab)
    h = h_ref[...]
    for k in range(bn // _SUB):
        acc = _dot(h, w_ref[:, k * _SUB:(k + 1) * _SUB])
        for hh in range(_SUB // LANES):
            x = acc[:, hh * LANES:(hh + 1) * LANES]
            if mode == "rope_a":
                x = x * tab_ref[base] + pltpu.roll(x, ROT_DIM // 2, 1) * tab_ref[base + 1] \
                    + pltpu.roll(x, HEAD_DIM - ROT_DIM // 2, 1) * tab_ref[base + 2]
            elif mode == "rope_b":
                x = x * tab_ref[base] + pltpu.roll(x, RET_QK_DIM // 2, 1) * tab_ref[base + 1]
            col = k * _SUB + hh * LANES
            o_ref[:, col:col + LANES] = x.astype(o_ref.dtype)


def _inproj(h, w, col0, ncols, name, mode="plain", tabs=None, q_blocks=0, seq=None,
            bm=1024, bn=1024):
    t, d = h.shape
    cb0 = col0 // bn
    in_specs = [pl.BlockSpec((bm, d), lambda i, j: (i, 0)),
                pl.BlockSpec((d, bn), lambda i, j: (0, cb0 + j))]
    args = [h, w]
    if mode != "plain":
        sblocks = seq // bm
        in_specs.append(pl.BlockSpec((tabs.shape[0], bm, LANES), lambda i, j: (0, i % sblocks, 0)))
        args.append(tabs)
    return pl.pallas_call(
        functools.partial(_inproj_kernel, mode=mode, bn=bn, q_blocks=q_blocks),
        grid=(t // bm, ncols // bn),
        in_specs=in_specs,
        out_specs=pl.BlockSpec((bm, bn), lambda i, j: (i, j)),
        out_shape=jax.ShapeDtypeStruct((t, ncols), BF16),
        compiler_params=_params(("parallel", "arbitrary")),
        name=name,
    )(*args)


def _rope_tables(seq):
    pos = jnp.arange(seq, dtype=F32)
    half = ROT_DIM // 2
    inv = ROPE_THETA ** (-jnp.arange(0, ROT_DIM, 2, dtype=F32) / ROT_DIM)
    ang = pos[:, None] * inv[None, :]
    cos, sin = jnp.cos(ang), jnp.sin(ang)
    zeros = jnp.zeros((seq, HEAD_DIM - ROT_DIM), F32)
    zh = jnp.zeros((seq, half), F32)
    c_a = jnp.concatenate([cos, cos, jnp.ones_like(zeros)], axis=-1)
    sa_a = jnp.concatenate([zh, sin, zeros], axis=-1)
    sb_a = jnp.concatenate([-sin, zh, zeros], axis=-1)
    qs = HEAD_DIM ** -0.5
    tab_a = jnp.stack([c_a * qs, sa_a * qs, sb_a * qs, c_a, sa_a, sb_a], axis=0)
    tab_a = tab_a.reshape(6, seq // NCLS, NCLS, LANES).transpose(0, 2, 1, 3).reshape(6, seq, LANES)
    invb = RET_ROPE_THETA ** (-jnp.arange(0, RET_QK_DIM, 2, dtype=F32) / RET_QK_DIM)
    angb = pos[:, None] * invb[None, :]
    cosb, sinb = jnp.cos(angb), jnp.sin(angb)
    c_b = jnp.concatenate([cosb, cosb], axis=-1)
    s_b = jnp.concatenate([-sinb, sinb], axis=-1)
    ks = RET_QK_DIM ** -0.5
    tab_b = jnp.stack([c_b, s_b, c_b * ks, s_b * ks], axis=0)
    return tab_a, tab_b


def _attention_masks():
    def table(pos_q, pos_k, first_from):
        dist = pos_q[:, None] - pos_k[None, :]
        ok = (dist >= 0) & (dist <= ATT_BLOCK)
        first = ok & (pos_k[None, :] >= first_from)
        return np.where(np.stack([ok, first]), 0.0, NEG_INF).astype(np.float32)

    i = np.arange(ATT_BLOCK)
    m16 = table(ATT_BLOCK + i, np.arange(2 * ATT_BLOCK), ATT_BLOCK)
    a, j = np.divmod(np.arange(ATT_BLOCK), 32)
    ak, jk = np.divmod(np.arange(2 * ATT_BLOCK), 64)
    m4 = table(ATT_BLOCK + 4 * j + a, 4 * jk + ak, ATT_BLOCK)
    r, j = np.divmod(np.arange(2 * ATT_BLOCK), 16)
    rk, jk = np.divmod(np.arange(4 * ATT_BLOCK), 32)
    m1 = table(2 * ATT_BLOCK + 16 * j + r, 16 * jk + rk, 2 * ATT_BLOCK)
    return jnp.asarray(m1), jnp.asarray(m4), jnp.asarray(m16)


def _online_block(q, k, v, bias, m_old, l_old, acc_old):
    nq, nk = q.shape[0], k.shape[0]
    s = _dot_nt(q, k) + bias
    m_blk = jnp.max(s, axis=1, keepdims=True)
    if m_old is None:
        m_new = jnp.broadcast_to(m_blk, (nq, LANES))
    else:
        m_new = jnp.maximum(m_old, m_blk)
    p = jnp.exp(s - jnp.concatenate([m_new] * (nk // LANES), axis=1))
    l_blk = jnp.sum(p, axis=1, keepdims=True)
    o = _dot(p.astype(BF16), v)
    if m_old is None:
        return m_new, jnp.broadcast_to(l_blk, (nq, LANES)), o
    alpha = jnp.exp(m_old - m_new)
    return m_new, alpha * l_old + l_blk, alpha * acc_old + o


_U16, _U4, _U1 = 8, 2, 4


def _attention_kernel(q1_ref, q4_ref, q16_ref, kc1_ref, kc4_ref, kc16_ref,
                      kp1_ref, kp4_ref, kp16_ref, vc1_ref, vc4_ref, vc16_ref,
                      vp1_ref, vp4_ref, vp16_ref, m1_ref, m4_ref, m16_ref,
                      o_ref, k4_ref, v4_ref, k1_ref, v1_ref, acc_ref, mst_ref, lst_ref):
    first_chunk = pl.program_id(1) == 0
    cat = lambda parts: jnp.concatenate(parts, axis=0)

    def body16(i, carry):
        bias = m16_ref[first_chunk.astype(jnp.int32)]
        res = []
        for u in range(_U16):
            r = i * _U16 + u
            k = cat([kp16_ref[r], kc16_ref[r]])
            v = cat([vp16_ref[r], vc16_ref[r]])
            res.append(_online_block(q16_ref[r], k, v, bias, None, None, None))
        for u, (m, l, o) in enumerate(res):
            r = i * _U16 + u
            mst_ref[r] = m
            lst_ref[r] = l
            acc_ref[r] = o
        return carry

    lax.fori_loop(0, NCLS // _U16, body16, 0)

    k4_ref[:, 0:32, :] = kp4_ref[...]
    k4_ref[:, 32:32 + ATT_BLOCK, :] = kc4_ref[...]
    v4_ref[:, 0:32, :] = vp4_ref[...]
    v4_ref[:, 32:32 + ATT_BLOCK, :] = vc4_ref[...]

    def body4(i, carry):
        loaded = []
        for u in range(_U4):
            nn = i * _U4 + u
            qrows = pl.ds(pl.multiple_of(nn * 32, 32), 32)
            krows = pl.ds(pl.multiple_of(nn * 32, 32), 64)
            bias = m4_ref[jnp.logical_and(first_chunk, nn == 0).astype(jnp.int32)]
            for r4 in range(4):
                cls = [r4 + 4 * a for a in range(4)]
                loaded.append((cat([q4_ref[cl, qrows, :] for cl in cls]),
                               cat([k4_ref[cl, krows, :] for cl in cls]),
                               cat([v4_ref[cl, krows, :] for cl in cls]), bias,
                               cat([mst_ref[cl, qrows, :] for cl in cls]),
                               cat([lst_ref[cl, qrows, :] for cl in cls]),
                               cat([acc_ref[cl, qrows, :] for cl in cls])))
        res = [_online_block(*args) for args in loaded]
        for n, (m, l, o) in enumerate(res):
            u, r4 = divmod(n, 4)
            qrows = pl.ds(pl.multiple_of((i * _U4 + u) * 32, 32), 32)
            for a in range(4):
                cl = r4 + 4 * a
                part = slice(32 * a, 32 * (a + 1))
                mst_ref[cl, qrows, :] = m[part]
                lst_ref[cl, qrows, :] = l[part]
                acc_ref[cl, qrows, :] = o[part]
        return carry

    lax.fori_loop(0, 4 // _U4, body4, 0)

    k1_ref[:, 0:16, :] = kp1_ref[...]
    k1_ref[:, 16:16 + ATT_BLOCK, :] = kc1_ref[...]
    v1_ref[:, 0:16, :] = vp1_ref[...]
    v1_ref[:, 16:16 + ATT_BLOCK, :] = vc1_ref[...]

    def body1(i, carry):
        loaded = []
        for u in range(_U1):
            mb = i * _U1 + u
            qrows = pl.ds(pl.multiple_of(mb * 16, 16), 16)
            krows = pl.ds(pl.multiple_of(mb * 16, 16), 32)
            bias = m1_ref[jnp.logical_and(first_chunk, mb == 0).astype(jnp.int32)]
            rng = range(NCLS)
            loaded.append((cat([q1_ref[cl, qrows, :] for cl in rng]),
                           cat([k1_ref[cl, krows, :] for cl in rng]),
                           cat([v1_ref[cl, krows, :] for cl in rng]), bias,
                           cat([mst_ref[cl, qrows, :] for cl in rng]),
                           cat([lst_ref[cl, qrows, :] for cl in rng]),
                           cat([acc_ref[cl, qrows, :] for cl in rng])))
        res = [_online_block(*args) for args in loaded]
        for u, (_, l, o) in enumerate(res):
            qrows = pl.ds(pl.multiple_of((i * _U1 + u) * 16, 16), 16)
            out = o / l
            for cl in range(NCLS):
                o_ref[cl, qrows, :] = out[16 * cl:16 * (cl + 1)].astype(o_ref.dtype)
        return carry

    lax.fori_loop(0, 8 // _U1, body1, 0)


def _attention(qk, v, masks, batch, seq):
    lcls = seq // NCLS
    nchunk = seq // CHUNK
    nh = A_HEADS
    m1, m4, m16 = masks

    def cur(colbase):
        return pl.BlockSpec((None, NCLS, ATT_BLOCK, HEAD_DIM),
                            lambda b, c, h: (b, 0, c, colbase + h))

    def prev(colbase, rows):
        per = ATT_BLOCK // rows
        return pl.BlockSpec((None, NCLS, rows, HEAD_DIM),
                            lambda b, c, h: (b, 0, jnp.maximum(c * per - 1, 0), colbase + h))

    def const(m):
        return pl.BlockSpec(m.shape, lambda b, c, h: (0, 0, 0))

    kb = 3 * nh
    in_specs = [cur(0), cur(nh), cur(2 * nh),
                cur(kb), cur(kb + nh), cur(kb + 2 * nh),
                prev(kb, 16), prev(kb + nh, 32), prev(kb + 2 * nh, ATT_BLOCK),
                cur(0), cur(nh), cur(2 * nh),
                prev(0, 16), prev(nh, 32), prev(2 * nh, ATT_BLOCK),
                const(m1), const(m4), const(m16)]
    return pl.pallas_call(
        _attention_kernel,
        grid=(batch, nchunk, nh),
        in_specs=in_specs,
        out_specs=pl.BlockSpec((None, NCLS, ATT_BLOCK, HEAD_DIM), lambda b, c, h: (b, 0, c, h)),
        out_shape=jax.ShapeDtypeStruct((batch, NCLS, lcls, A_WIDTH), BF16),
        scratch_shapes=[pltpu.VMEM((NCLS, 32 + ATT_BLOCK, HEAD_DIM), BF16),
                        pltpu.VMEM((NCLS, 32 + ATT_BLOCK, HEAD_DIM), BF16),
                        pltpu.VMEM((NCLS, 16 + ATT_BLOCK, HEAD_DIM), BF16),
                        pltpu.VMEM((NCLS, 16 + ATT_BLOCK, HEAD_DIM), BF16),
                        pltpu.VMEM((NCLS, ATT_BLOCK, HEAD_DIM), F32),
                        pltpu.VMEM((NCLS, ATT_BLOCK, HEAD_DIM), F32),
                        pltpu.VMEM((NCLS, ATT_BLOCK, HEAD_DIM), F32)],
        compiler_params=_params(("parallel", "parallel", "parallel")),
        name="dilated_attention",
    )(*([qk] * 9), *([v] * 6), m1, m4, m16)


def _retention_kernel(cd_ref, q_ref, k_ref, v0_ref, v1_ref, dm_ref, kd_ref, qd_ref,
                      o_ref, st_ref):
    @pl.when(pl.program_id(1) == 0)
    def _():
        st_ref[...] = jnp.zeros_like(st_ref)

    for h in range(RET_HEADS):
        cols = slice(h * RET_QK_DIM, (h + 1) * RET_QK_DIM)
        q = q_ref[:, cols]
        k = k_ref[:, cols]
        v_ref = v0_ref if h < RET_HEADS // 2 else v1_ref
        hv = h % (RET_HEADS // 2)
        v = v_ref[:, hv * RET_V_DIM:(hv + 1) * RET_V_DIM]
        state = st_ref[h]
        scores = _dot_nt(q, k) * dm_ref[h]
        intra = _dot(scores.astype(BF16), v)
        q_dec = (q.astype(F32) * qd_ref[h]).astype(BF16)
        cross = _dot(q_dec, state.astype(BF16))
        out = intra + cross
        ms = jnp.mean(out * out, axis=-1, keepdims=True)
        o_ref[:, h * RET_V_DIM:(h + 1) * RET_V_DIM] = (out * lax.rsqrt(ms + EPS)).astype(o_ref.dtype)
        k_dec = (k.astype(F32) * kd_ref[h]).astype(BF16)
        st_ref[h] = state * cd_ref[h] + _dot_tn(k_dec, v)


def _retention_tables():
    c = RET_CHUNK
    log_g = jnp.log1p(-jnp.power(2.0, -5.0 - jnp.arange(RET_HEADS, dtype=F32)))
    idx = jnp.arange(c, dtype=F32)
    diff = idx[:, None] - idx[None, :]
    dmask = jnp.where(diff >= 0, jnp.exp(jnp.maximum(diff, 0.0)[None] * log_g[:, None, None]), 0.0)
    kd = jnp.exp((c - 1 - idx)[None, :] * log_g[:, None])
    qd = jnp.exp((idx + 1.0)[None, :] * log_g[:, None])
    cd = jnp.exp(c * log_g)
    rep = lambda t: jnp.broadcast_to(t[:, :, None], (RET_HEADS, c, RET_QK_DIM))
    return cd, dmask, rep(kd), rep(qd)


def _retention(bqk, bvg, tables, batch, seq):
    cd, dmask, kd, qd = tables
    t = bqk.shape[0]
    nchunk = seq // RET_CHUNK
    w = RET_HEADS * RET_QK_DIM

    def rows(col):
        return pl.BlockSpec((RET_CHUNK, w), lambda b, n: (b * nchunk + n, col))

    def const(a):
        return pl.BlockSpec(a.shape, lambda b, n: (0, 0, 0))

    return pl.pallas_call(
        _retention_kernel,
        grid=(batch, nchunk),
        in_specs=[pl.BlockSpec(memory_space=pltpu.SMEM),
                  rows(0), rows(1), rows(0), rows(1),
                  const(dmask), const(kd), const(qd)],
        out_specs=pl.BlockSpec((RET_CHUNK, RET_HEADS * RET_V_DIM), lambda b, n: (b * nchunk + n, 0)),
        out_shape=jax.ShapeDtypeStruct((t, RET_HEADS * RET_V_DIM), BF16),
        scratch_shapes=[pltpu.VMEM((RET_HEADS, RET_QK_DIM, RET_V_DIM), F32)],
        compiler_params=_params(("parallel", "arbitrary")),
        name="retention",
    )(cd, bqk, bqk, bvg, bvg, dmask, kd, qd)


def _even_out_kernel(a_ref, r_ref, g0_ref, g1_ref, g2_ref, x_ref, w_ref, ng_ref,
                     x1_ref, h2_ref):
    def gated(y, g_ref):
        g = g_ref[...].astype(F32)
        return (y.astype(F32) * (g * _sigmoid(g))).astype(BF16)

    acc = _dot(gated(a_ref[...], g0_ref), w_ref[0:1024, :])
    acc += _dot(gated(r_ref[:, 0:1024], g1_ref), w_ref[1024:2048, :])
    acc += _dot(gated(r_ref[:, 1024:2048], g2_ref), w_ref[2048:3072, :])
    x1 = x_ref[...] + acc
    x1_ref[...] = x1
    h2_ref[...] = _rms(x1, ng_ref[...]).astype(h2_ref.dtype)


def _even_out(a, r, bvg, x, w, ng, bm=256):
    t = x.shape[0]
    gcol = 2
    row = lambda width: pl.BlockSpec((bm, width), lambda i: (i, 0))
    gate = lambda col: pl.BlockSpec((bm, 1024), lambda i: (i, col))
    return pl.pallas_call(
        _even_out_kernel,
        grid=(t // bm,),
        in_specs=[row(A_WIDTH), row(RET_HEADS * RET_V_DIM),
                  gate(gcol), gate(gcol + 1), gate(gcol + 2),
                  row(D_MODEL),
                  pl.BlockSpec(w.shape, lambda i: (0, 0), pipeline_mode=pl.Buffered(1)),
                  pl.BlockSpec((1, D_MODEL), lambda i: (0, 0))],
        out_specs=[row(D_MODEL), row(D_MODEL)],
        out_shape=[jax.ShapeDtypeStruct((t, D_MODEL), F32),
                   jax.ShapeDtypeStruct((t, D_MODEL), BF16)],
        compiler_params=_params(("parallel",)),
        name="even_outproj",
    )(a, r, bvg, bvg, bvg, x, w, ng)


def _odd_inproj_kernel(h_ref, wa_ref, wb_ref, wg_ref, ba_ref, bb_ref, bg_ref, u_ref, sg_ref, *, bn):
    h = h_ref[...]
    for k in range(bn // _SUB):
        cols = slice(k * _SUB, (k + 1) * _SUB)
        a = _dot(h, wa_ref[:, cols]) + ba_ref[:, cols]
        b = _dot(h, wb_ref[:, cols]) + bb_ref[:, cols]
        u_ref[:, cols] = (a * _sigmoid(b)).astype(u_ref.dtype)
        g = _dot(h, wg_ref[:, cols]) + bg_ref[:, cols]
        sg_ref[:, cols] = (g * _sigmoid(g)).astype(sg_ref.dtype)


def _odd_inproj(h, w, b, bm=1024, bn=512):
    t, d = h.shape
    nb = CONV_WIDTH // bn
    wspec = lambda k: pl.BlockSpec((d, bn), lambda i, j: (0, j + k * nb))
    bspec = lambda k: pl.BlockSpec((1, bn), lambda i, j: (0, j + k * nb))
    ospec = pl.BlockSpec((bm, bn), lambda i, j: (i, j))
    return pl.pallas_call(
        functools.partial(_odd_inproj_kernel, bn=bn),
        grid=(t // bm, nb),
        in_specs=[pl.BlockSpec((bm, d), lambda i, j: (i, 0)),
                  wspec(0), wspec(1), wspec(2), bspec(0), bspec(1), bspec(2)],
        out_specs=[ospec, ospec],
        out_shape=[jax.ShapeDtypeStruct((t, CONV_WIDTH), BF16)] * 2,
        compiler_params=_params(("parallel", "arbitrary")),
        name="odd_inproj",
    )(h, w, w, w, b, b, b)


_CONV_ROWS = 64
_TAP_LEAD = HALO - (CONV_KERNEL - 1)


def _conv_kernel(u_ref, w_ref, b_ref, o_ref, ext_ref, *, bt):
    bc = ext_ref.shape[1]

    @pl.when(pl.program_id(2) == 0)
    def _():
        ext_ref[0:HALO, :] = jnp.zeros((HALO, bc), F32)

    ext_ref[HALO:HALO + bt, :] = u_ref[...].astype(F32)
    nrow = _CONV_ROWS + HALO

    def body(rc, carry):
        r0 = pl.multiple_of(rc * _CONV_ROWS, _CONV_ROWS)
        for lg in range(bc // LANES):
            cols = slice(lg * LANES, (lg + 1) * LANES)
            e = ext_ref[pl.ds(r0, nrow), cols]
            acc = jnp.broadcast_to(b_ref[:, cols], (_CONV_ROWS, LANES))
            for s in range(SUBLANES):
                es = e if s == 0 else pltpu.roll(e, nrow - s, 0)
                for q in range(HALO // SUBLANES + 1):
                    k = SUBLANES * q + s - _TAP_LEAD
                    if 0 <= k < CONV_KERNEL:
                        acc = acc + w_ref[k:k + 1, cols] * es[SUBLANES * q:SUBLANES * q + _CONV_ROWS]
            o_ref[pl.ds(r0, _CONV_ROWS), cols] = acc.astype(o_ref.dtype)
        return carry

    lax.fori_loop(0, bt // _CONV_ROWS, body, 0)
    ext_ref[0:HALO, :] = ext_ref[bt:bt + HALO, :]


def _conv(u, w, b, batch, seq, bt=512, bc=512):
    t = u.shape[0]
    nt = seq // bt
    return pl.pallas_call(
        functools.partial(_conv_kernel, bt=bt),
        grid=(batch, CONV_WIDTH // bc, nt),
        in_specs=[pl.BlockSpec((bt, bc), lambda bb, c, i: (bb * nt + i, c)),
                  pl.BlockSpec((HALO, bc), lambda bb, c, i: (0, c)),
                  pl.BlockSpec((1, bc), lambda bb, c, i: (0, c))],
        out_specs=pl.BlockSpec((bt, bc), lambda bb, c, i: (bb * nt + i, c)),
        out_shape=jax.ShapeDtypeStruct((t, CONV_WIDTH), BF16),
        scratch_shapes=[pltpu.VMEM((bt + HALO, bc), F32)],
        compiler_params=_params(("parallel", "parallel", "arbitrary")),
        name="causal_conv",
    )(u, w, b)


def _odd_out_kernel(c_ref, sg_ref, lg_ref, lb_ref, w_ref, x_ref, bo_ref, fg_ref, o_ref):
    c = c_ref[...].astype(F32)
    mu = jnp.mean(c, axis=-1, keepdims=True)
    cen = c - mu
    var = jnp.mean(cen * cen, axis=-1, keepdims=True)
    y = cen * lax.rsqrt(var + EPS) * lg_ref[...] + lb_ref[...]
    z = (y * _sigmoid(y)) * sg_ref[...].astype(F32)
    x2 = x_ref[...] + _dot(z.astype(BF16), w_ref[...]) + bo_ref[...]
    o_ref[...] = _rms(x2, fg_ref[...])


def _odd_out(c, sg, lg, lb, w, x, bo, fg, bm=256):
    t = x.shape[0]
    row = lambda width: pl.BlockSpec((bm, width), lambda i: (i, 0))
    vec = lambda width: pl.BlockSpec((1, width), lambda i: (0, 0))
    return pl.pallas_call(
        _odd_out_kernel,
        grid=(t // bm,),
        in_specs=[row(CONV_WIDTH), row(CONV_WIDTH), vec(CONV_WIDTH), vec(CONV_WIDTH),
                  pl.BlockSpec(w.shape, lambda i: (0, 0), pipeline_mode=pl.Buffered(1)),
                  row(D_MODEL), vec(D_MODEL), vec(D_MODEL)],
        out_specs=row(D_MODEL),
        out_shape=jax.ShapeDtypeStruct((t, D_MODEL), F32),
        compiler_params=_params(("parallel",)),
        name="odd_outproj",
    )(c, sg, lg, lb, w, x, bo, fg)


def _trunk(x, norm_even, w_in_even, w_out_even, norm_odd, w_in_odd, b_in_odd, conv_w_odd,
           conv_b_odd, ln_g_odd, ln_b_odd, w_out_odd, b_out_odd, final_norm):
    batch, seq, d = x.shape
    t = batch * seq
    row = lambda v: v.reshape(1, -1)
    x2d = x.reshape(t, d)

    h, hc = _rmsnorm(x2d, row(norm_even[0]), batch, seq)
    hc = hc.reshape(t, d)
    w_in = w_in_even[0].astype(BF16)
    tab_a, tab_b = _rope_tables(seq)
    qk_w, v_w = 6 * A_WIDTH, 3 * A_WIDTH
    bqk_w = 2 * RET_HEADS * RET_QK_DIM
    aqk = _inproj(hc, w_in, 0, qk_w, "even_inproj_aqk", "rope_a", tab_a, 3, seq)
    av = _inproj(hc, w_in, qk_w, v_w, "even_inproj_av")
    bqk = _inproj(h, w_in, qk_w + v_w, bqk_w, "even_inproj_bqk", "rope_b", tab_b, 1, seq)
    bvg = _inproj(h, w_in, qk_w + v_w + bqk_w, w_in.shape[1] - qk_w - v_w - bqk_w, "even_inproj_bvg")
    lcls = seq // NCLS
    a16 = _attention(aqk.reshape(batch, NCLS, lcls, qk_w), av.reshape(batch, NCLS, lcls, v_w),
                     _attention_masks(), batch, seq)
    a = a16.transpose(0, 2, 1, 3).reshape(t, A_WIDTH)
    r = _retention(bqk, bvg, _retention_tables(), batch, seq)
    x1, h2 = _even_out(a, r, bvg, x2d, w_out_even[0].astype(BF16), row(norm_odd[0]))

    u, sg = _odd_inproj(h2, w_in_odd[0].astype(BF16), row(b_in_odd[0]))
    cw = jnp.concatenate([conv_w_odd[0], jnp.zeros((HALO - CONV_KERNEL, CONV_WIDTH), F32)], axis=0)
    c = _conv(u, cw, row(conv_b_odd[0]), batch, seq)
    out = _odd_out(c, sg, row(ln_g_odd[0]), row(ln_b_odd[0]), w_out_odd[0].astype(BF16),
                   x1, row(b_out_odd[0]), row(final_norm))
    return out.reshape(batch, seq, d)


def kernel(x, norm_even, w_in_even, w_out_even, norm_odd, w_in_odd, b_in_odd, conv_w_odd,
           conv_b_odd, ln_g_odd, ln_b_odd, w_out_odd, b_out_odd, final_norm):
    return _trunk(x, norm_even, w_in_even, w_out_even, norm_odd, w_in_odd, b_in_odd,
                  conv_w_odd, conv_b_odd, ln_g_odd, ln_b_odd, w_out_odd, b_out_odd, final_norm)
```

```python
import functools

import numpy as np
import jax
import jax.numpy as jnp
from jax import lax
from jax.experimental import pallas as pl
from jax.experimental.pallas import tpu as pltpu

F32 = jnp.float32
BF16 = jnp.bfloat16

D_MODEL = 2048
A_HEADS = 8
HEAD_DIM = 128
ATT_BLOCK = 128
ROPE_THETA = 500000.0
ROT_DIM = HEAD_DIM // 4
A_WIDTH = A_HEADS * HEAD_DIM
RET_HEADS = 8
RET_QK_DIM = 128
RET_V_DIM = 256
RET_CHUNK = 128
RET_ROPE_THETA = 10000.0
CONV_WIDTH = 4096
CONV_KERNEL = 31
EPS = 1e-6
NEG_INF = -1e30

NCLS = 16
CHUNK = NCLS * ATT_BLOCK
HALO = 32
LANES = 128
SUBLANES = 8

VMEM_LIMIT_V7X = 56 * 1024 * 1024


def _params(sem):
    return pltpu.CompilerParams(dimension_semantics=sem, vmem_limit_bytes=VMEM_LIMIT_V7X)


def _dot(a, b):
    return jnp.dot(a, b, preferred_element_type=F32)


def _dot_nt(a, b):
    return lax.dot_general(a, b, (((1,), (1,)), ((), ())), preferred_element_type=F32)


def _dot_tn(a, b):
    return lax.dot_general(a, b, (((0,), (0,)), ((), ())), preferred_element_type=F32)


def _sigmoid(x):
    return 1.0 / (1.0 + jnp.exp(-x))


def _rms(x, g):
    ms = jnp.mean(x * x, axis=-1, keepdims=True)
    return x * lax.rsqrt(ms + EPS) * g


def _rmsnorm_kernel(x_ref, g_ref, p_ref, h_ref, hc_ref, *, bl):
    h = _rms(x_ref[...], g_ref[...]).astype(h_ref.dtype)
    h_ref[...] = h
    hc = _dot(p_ref[...], h).astype(hc_ref.dtype)
    for r in range(NCLS):
        hc_ref[r] = hc[r * bl:(r + 1) * bl]


def _class_major_perm(bl):
    n = bl * NCLS
    out_row = np.arange(n)
    src = NCLS * (out_row % bl) + out_row // bl
    return jnp.asarray((src[:, None] == np.arange(n)[None, :]).astype(np.float32), dtype=BF16)


def _rmsnorm(x, g, batch, seq, bl=32):
    t, d = x.shape
    lcls = seq // NCLS
    nb = lcls // bl
    perm = _class_major_perm(bl)
    return pl.pallas_call(
        functools.partial(_rmsnorm_kernel, bl=bl),
        grid=(batch, nb),
        in_specs=[pl.BlockSpec((bl * NCLS, d), lambda b, i: (b * nb + i, 0)),
                  pl.BlockSpec((1, d), lambda b, i: (0, 0)),
                  pl.BlockSpec(perm.shape, lambda b, i: (0, 0))],
        out_specs=[pl.BlockSpec((bl * NCLS, d), lambda b, i: (b * nb + i, 0)),
                   pl.BlockSpec((None, NCLS, bl, d), lambda b, i: (b, 0, i, 0))],
        out_shape=[jax.ShapeDtypeStruct((t, d), BF16),
                   jax.ShapeDtypeStruct((batch, NCLS, lcls, d), BF16)],
        compiler_params=_params(("parallel", "parallel")),
        name="rmsnorm_even",
    )(x, g, perm)


_SUB = 256


def _inproj_kernel(h_ref, w_ref, *rest, mode, bn, q_blocks):
    if mode == "plain":
        o_ref, wb_ref = rest
    else:
        tab_ref, o_ref, wb_ref = rest
        ntab = 3 if mode == "rope_a" else 2
        base = jnp.where(pl.program_id(0) < q_blocks, 0, ntab)

    @pl.when(pl.program_id(1) == 0)
    def _():
        wb_ref[...] = w_ref[...].astype(wb_ref.dtype)

    h = h_ref[...]
    for k in range(bn // _SUB):
        acc = _dot(h, wb_ref[:, k * _SUB:(k + 1) * _SUB])
        for hh in range(_SUB // LANES):
            x = acc[:, hh * LANES:(hh + 1) * LANES]
            if mode == "rope_a":
                x = x * tab_ref[base] + pltpu.roll(x, ROT_DIM // 2, 1) * tab_ref[base + 1] \
                    + pltpu.roll(x, HEAD_DIM - ROT_DIM // 2, 1) * tab_ref[base + 2]
            elif mode == "rope_b":
                x = x * tab_ref[base] + pltpu.roll(x, RET_QK_DIM // 2, 1) * tab_ref[base + 1]
            col = k * _SUB + hh * LANES
            o_ref[:, col:col + LANES] = x.astype(o_ref.dtype)


def _inproj(h, w, col0, ncols, name, mode="plain", tabs=None, q_blocks=0, seq=None,
            bm=1024, bn=1024):
    t, d = h.shape
    cb0 = col0 // bn
    in_specs = [pl.BlockSpec((bm, d), lambda j, i: (i, 0)),
                pl.BlockSpec((d, bn), lambda j, i: (0, cb0 + j))]
    args = [h, w]
    if mode != "plain":
        sblocks = seq // bm
        in_specs.append(pl.BlockSpec((tabs.shape[0], bm, LANES), lambda j, i: (0, i % sblocks, 0)))
        args.append(tabs)
    return pl.pallas_call(
        functools.partial(_inproj_kernel, mode=mode, bn=bn, q_blocks=q_blocks),
        grid=(ncols // bn, t // bm),
        in_specs=in_specs,
        out_specs=pl.BlockSpec((bm, bn), lambda j, i: (i, j)),
        out_shape=jax.ShapeDtypeStruct((t, ncols), BF16),
        scratch_shapes=[pltpu.VMEM((d, bn), BF16)],
        compiler_params=_params(("arbitrary", "arbitrary")),
        name=name,
    )(*args)


def _rope_tables(seq):
    pos = jnp.arange(seq, dtype=F32)
    half = ROT_DIM // 2
    inv = ROPE_THETA ** (-jnp.arange(0, ROT_DIM, 2, dtype=F32) / ROT_DIM)
    ang = pos[:, None] * inv[None, :]
    cos, sin = jnp.cos(ang), jnp.sin(ang)
    zeros = jnp.zeros((seq, HEAD_DIM - ROT_DIM), F32)
    zh = jnp.zeros((seq, half), F32)
    c_a = jnp.concatenate([cos, cos, jnp.ones_like(zeros)], axis=-1)
    sa_a = jnp.concatenate([zh, sin, zeros], axis=-1)
    sb_a = jnp.concatenate([-sin, zh, zeros], axis=-1)
    qs = HEAD_DIM ** -0.5
    tab_a = jnp.stack([c_a * qs, sa_a * qs, sb_a * qs, c_a, sa_a, sb_a], axis=0)
    tab_a = tab_a.reshape(6, seq // NCLS, NCLS, LANES).transpose(0, 2, 1, 3).reshape(6, seq, LANES)
    invb = RET_ROPE_THETA ** (-jnp.arange(0, RET_QK_DIM, 2, dtype=F32) / RET_QK_DIM)
    angb = pos[:, None] * invb[None, :]
    cosb, sinb = jnp.cos(angb), jnp.sin(angb)
    c_b = jnp.concatenate([cosb, cosb], axis=-1)
    s_b = jnp.concatenate([-sinb, sinb], axis=-1)
    ks = RET_QK_DIM ** -0.5
    tab_b = jnp.stack([c_b, s_b, c_b * ks, s_b * ks], axis=0)
    return tab_a, tab_b


def _attention_masks():
    def table(pos_q, pos_k, first_from):
        dist = pos_q[:, None] - pos_k[None, :]
        ok = (dist >= 0) & (dist <= ATT_BLOCK)
        first = ok & (pos_k[None, :] >= first_from)
        return np.where(np.stack([ok, first]), 0.0, NEG_INF).astype(np.float32)

    i = np.arange(ATT_BLOCK)
    m16 = table(ATT_BLOCK + i, np.arange(2 * ATT_BLOCK), ATT_BLOCK)
    a, j = np.divmod(np.arange(ATT_BLOCK), 32)
    ak, jk = np.divmod(np.arange(2 * ATT_BLOCK), 64)
    m4 = table(ATT_BLOCK + 4 * j + a, 4 * jk + ak, ATT_BLOCK)
    r, j = np.divmod(np.arange(2 * ATT_BLOCK), 16)
    rk, jk = np.divmod(np.arange(4 * ATT_BLOCK), 32)
    m1 = table(2 * ATT_BLOCK + 16 * j + r, 16 * jk + rk, 2 * ATT_BLOCK)
    return jnp.asarray(m1), jnp.asarray(m4), jnp.asarray(m16)


def _online_block(q, k, v, bias, m_old, l_old, acc_old):
    nq, nk = q.shape[0], k.shape[0]
    s = _dot_nt(q, k) + bias
    m_blk = jnp.max(s, axis=1, keepdims=True)
    if m_old is None:
        m_new = jnp.broadcast_to(m_blk, (nq, LANES))
    else:
        m_new = jnp.maximum(m_old, m_blk)
    p = jnp.exp(s - jnp.concatenate([m_new] * (nk // LANES), axis=1))
    l_blk = jnp.sum(p, axis=1, keepdims=True)
    o = _dot(p.astype(BF16), v)
    if m_old is None:
        return m_new, jnp.broadcast_to(l_blk, (nq, LANES)), o
    alpha = jnp.exp(m_old - m_new)
    return m_new, alpha * l_old + l_blk, alpha * acc_old + o


_U16, _U4, _U1 = 8, 2, 4


def _attention_kernel(q1_ref, q4_ref, q16_ref, kc1_ref, kc4_ref, kc16_ref,
                      kp1_ref, kp4_ref, kp16_ref, vc1_ref, vc4_ref, vc16_ref,
                      vp1_ref, vp4_ref, vp16_ref, m1_ref, m4_ref, m16_ref,
                      o_ref, k4_ref, v4_ref, k1_ref, v1_ref, acc_ref, mst_ref, lst_ref):
    first_chunk = pl.program_id(1) == 0
    cat = lambda parts: jnp.concatenate(parts, axis=0)

    def body16(i, carry):
        bias = m16_ref[first_chunk.astype(jnp.int32)]
        res = []
        for u in range(_U16):
            r = i * _U16 + u
            k = cat([kp16_ref[r], kc16_ref[r]])
            v = cat([vp16_ref[r], vc16_ref[r]])
            res.append(_online_block(q16_ref[r], k, v, bias, None, None, None))
        for u, (m, l, o) in enumerate(res):
            r = i * _U16 + u
            mst_ref[r] = m
            lst_ref[r] = l
            acc_ref[r] = o
        return carry

    lax.fori_loop(0, NCLS // _U16, body16, 0)

    k4_ref[:, 0:32, :] = kp4_ref[...]
    k4_ref[:, 32:32 + ATT_BLOCK, :] = kc4_ref[...]
    v4_ref[:, 0:32, :] = vp4_ref[...]
    v4_ref[:, 32:32 + ATT_BLOCK, :] = vc4_ref[...]

    def body4(i, carry):
        loaded = []
        for u in range(_U4):
            nn = i * _U4 + u
            qrows = pl.ds(pl.multiple_of(nn * 32, 32), 32)
            krows = pl.ds(pl.multiple_of(nn * 32, 32), 64)
            bias = m4_ref[jnp.logical_and(first_chunk, nn == 0).astype(jnp.int32)]
            for r4 in range(4):
                cls = [r4 + 4 * a for a in range(4)]
                loaded.append((cat([q4_ref[cl, qrows, :] for cl in cls]),
                               cat([k4_ref[cl, krows, :] for cl in cls]),
                               cat([v4_ref[cl, krows, :] for cl in cls]), bias,
                               cat([mst_ref[cl, qrows, :] for cl in cls]),
                               cat([lst_ref[cl, qrows, :] for cl in cls]),
                               cat([acc_ref[cl, qrows, :] for cl in cls])))
        res = [_online_block(*args) for args in loaded]
        for n, (m, l, o) in enumerate(res):
            u, r4 = divmod(n, 4)
            qrows = pl.ds(pl.multiple_of((i * _U4 + u) * 32, 32), 32)
            for a in range(4):
                cl = r4 + 4 * a
                part = slice(32 * a, 32 * (a + 1))
                mst_ref[cl, qrows, :] = m[part]
                lst_ref[cl, qrows, :] = l[part]
                acc_ref[cl, qrows, :] = o[part]
        return carry

    lax.fori_loop(0, 4 // _U4, body4, 0)

    k1_ref[:, 0:16, :] = kp1_ref[...]
    k1_ref[:, 16:16 + ATT_BLOCK, :] = kc1_ref[...]
    v1_ref[:, 0:16, :] = vp1_ref[...]
    v1_ref[:, 16:16 + ATT_BLOCK, :] = vc1_ref[...]

    def body1(i, carry):
        loaded = []
        for u in range(_U1):
            mb = i * _U1 + u
            qrows = pl.ds(pl.multiple_of(mb * 16, 16), 16)
            krows = pl.ds(pl.multiple_of(mb * 16, 16), 32)
            bias = m1_ref[jnp.logical_and(first_chunk, mb == 0).astype(jnp.int32)]
            rng = range(NCLS)
            loaded.append((cat([q1_ref[cl, qrows, :] for cl in rng]),
                           cat([k1_ref[cl, krows, :] for cl in rng]),
                           cat([v1_ref[cl, krows, :] for cl in rng]), bias,
                           cat([mst_ref[cl, qrows, :] for cl in rng]),
                           cat([lst_ref[cl, qrows, :] for cl in rng]),
                           cat([acc_ref[cl, qrows, :] for cl in rng])))
        res = [_online_block(*args) for args in loaded]
        for u, (_, l, o) in enumerate(res):
            qrows = pl.ds(pl.multiple_of((i * _U1 + u) * 16, 16), 16)
            out = o / l
            for cl in range(NCLS):
                o_ref[cl, qrows, :] = out[16 * cl:16 * (cl + 1)].astype(o_ref.dtype)
        return carry

    lax.fori_loop(0, 8 // _U1, body1, 0)


def _attention(qk, v, masks, batch, seq):
    lcls = seq // NCLS
    nchunk = seq // CHUNK
    nh = A_HEADS
    m1, m4, m16 = masks

    def cur(colbase):
        return pl.BlockSpec((None, NCLS, ATT_BLOCK, HEAD_DIM),
                            lambda b, c, h: (b, 0, c, colbase + h))

    def prev(colbase, rows):
        per = ATT_BLOCK // rows
        return pl.BlockSpec((None, NCLS, rows, HEAD_DIM),
                            lambda b, c, h: (b, 0, jnp.maximum(c * per - 1, 0), colbase + h))

    def const(m):
        return pl.BlockSpec(m.shape, lambda b, c, h: (0, 0, 0))

    kb = 3 * nh
    in_specs = [cur(0), cur(nh), cur(2 * nh),
                cur(kb), cur(kb + nh), cur(kb + 2 * nh),
                prev(kb, 16), prev(kb + nh, 32), prev(kb + 2 * nh, ATT_BLOCK),
                cur(0), cur(nh), cur(2 * nh),
                prev(0, 16), prev(nh, 32), prev(2 * nh, ATT_BLOCK),
                const(m1), const(m4), const(m16)]
    return pl.pallas_call(
        _attention_kernel,
        grid=(batch, nchunk, nh),
        in_specs=in_specs,
        out_specs=pl.BlockSpec((None, NCLS, ATT_BLOCK, HEAD_DIM), lambda b, c, h: (b, 0, c, h)),
        out_shape=jax.ShapeDtypeStruct((batch, NCLS, lcls, A_WIDTH), BF16),
        scratch_shapes=[pltpu.VMEM((NCLS, 32 + ATT_BLOCK, HEAD_DIM), BF16),
                        pltpu.VMEM((NCLS, 32 + ATT_BLOCK, HEAD_DIM), BF16),
                        pltpu.VMEM((NCLS, 16 + ATT_BLOCK, HEAD_DIM), BF16),
                        pltpu.VMEM((NCLS, 16 + ATT_BLOCK, HEAD_DIM), BF16),
                        pltpu.VMEM((NCLS, ATT_BLOCK, HEAD_DIM), F32),
                        pltpu.VMEM((NCLS, ATT_BLOCK, HEAD_DIM), F32),
                        pltpu.VMEM((NCLS, ATT_BLOCK, HEAD_DIM), F32)],
        compiler_params=_params(("parallel", "parallel", "parallel")),
        name="dilated_attention",
    )(*([qk] * 9), *([v] * 6), m1, m4, m16)


def _retention_kernel(cd_ref, q_ref, k_ref, v0_ref, v1_ref, dm_ref, kd_ref, qd_ref,
                      o_ref, st_ref):
    @pl.when(pl.program_id(1) == 0)
    def _():
        st_ref[...] = jnp.zeros_like(st_ref)

    for h in range(RET_HEADS):
        cols = slice(h * RET_QK_DIM, (h + 1) * RET_QK_DIM)
        q = q_ref[:, cols]
        k = k_ref[:, cols]
        v_ref = v0_ref if h < RET_HEADS // 2 else v1_ref
        hv = h % (RET_HEADS // 2)
        v = v_ref[:, hv * RET_V_DIM:(hv + 1) * RET_V_DIM]
        state = st_ref[h]
        scores = _dot_nt(q, k) * dm_ref[h]
        intra = _dot(scores.astype(BF16), v)
        q_dec = (q.astype(F32) * qd_ref[h]).astype(BF16)
        cross = _dot(q_dec, state.astype(BF16))
        out = intra + cross
        ms = jnp.mean(out * out, axis=-1, keepdims=True)
        o_ref[:, h * RET_V_DIM:(h + 1) * RET_V_DIM] = (out * lax.rsqrt(ms + EPS)).astype(o_ref.dtype)
        k_dec = (k.astype(F32) * kd_ref[h]).astype(BF16)
        st_ref[h] = state * cd_ref[h] + _dot_tn(k_dec, v)


def _retention_tables():
    c = RET_CHUNK
    log_g = jnp.log1p(-jnp.power(2.0, -5.0 - jnp.arange(RET_HEADS, dtype=F32)))
    idx = jnp.arange(c, dtype=F32)
    diff = idx[:, None] - idx[None, :]
    dmask = jnp.where(diff >= 0, jnp.exp(jnp.maximum(diff, 0.0)[None] * log_g[:, None, None]), 0.0)
    kd = jnp.exp((c - 1 - idx)[None, :] * log_g[:, None])
    qd = jnp.exp((idx + 1.0)[None, :] * log_g[:, None])
    cd = jnp.exp(c * log_g)
    rep = lambda t: jnp.broadcast_to(t[:, :, None], (RET_HEADS, c, RET_QK_DIM))
    return cd, dmask, rep(kd), rep(qd)


def _retention(bqk, bvg, tables, batch, seq):
    cd, dmask, kd, qd = tables
    t = bqk.shape[0]
    nchunk = seq // RET_CHUNK
    w = RET_HEADS * RET_QK_DIM

    def rows(col):
        return pl.BlockSpec((RET_CHUNK, w), lambda b, n: (b * nchunk + n, col))

    def const(a):
        return pl.BlockSpec(a.shape, lambda b, n: (0, 0, 0))

    return pl.pallas_call(
        _retention_kernel,
        grid=(batch, nchunk),
        in_specs=[pl.BlockSpec(memory_space=pltpu.SMEM),
                  rows(0), rows(1), rows(0), rows(1),
                  const(dmask), const(kd), const(qd)],
        out_specs=pl.BlockSpec((RET_CHUNK, RET_HEADS * RET_V_DIM), lambda b, n: (b * nchunk + n, 0)),
        out_shape=jax.ShapeDtypeStruct((t, RET_HEADS * RET_V_DIM), BF16),
        scratch_shapes=[pltpu.VMEM((RET_HEADS, RET_QK_DIM, RET_V_DIM), F32)],
        compiler_params=_params(("parallel", "arbitrary")),
        name="retention",
    )(cd, bqk, bqk, bvg, bvg, dmask, kd, qd)


def _even_out_kernel(a_ref, r_ref, g0_ref, g1_ref, g2_ref, x_ref, w_ref, ng_ref,
                     x1_ref, h2_ref):
    def gated(y, g_ref):
        g = g_ref[...].astype(F32)
        return (y.astype(F32) * (g * _sigmoid(g))).astype(BF16)

    acc = _dot(gated(a_ref[...], g0_ref), w_ref[0:1024, :])
    acc += _dot(gated(r_ref[:, 0:1024], g1_ref), w_ref[1024:2048, :])
    acc += _dot(gated(r_ref[:, 1024:2048], g2_ref), w_ref[2048:3072, :])
    x1 = x_ref[...] + acc
    x1_ref[...] = x1
    h2_ref[...] = _rms(x1, ng_ref[...]).astype(h2_ref.dtype)


def _even_out(a, r, bvg, x, w, ng, bm=256):
    t = x.shape[0]
    gcol = 2
    row = lambda width: pl.BlockSpec((bm, width), lambda i: (i, 0))
    gate = lambda col: pl.BlockSpec((bm, 1024), lambda i: (i, col))
    return pl.pallas_call(
        _even_out_kernel,
        grid=(t // bm,),
        in_specs=[row(A_WIDTH), row(RET_HEADS * RET_V_DIM),
                  gate(gcol), gate(gcol + 1), gate(gcol + 2),
                  row(D_MODEL),
                  pl.BlockSpec(w.shape, lambda i: (0, 0), pipeline_mode=pl.Buffered(1)),
                  pl.BlockSpec((1, D_MODEL), lambda i: (0, 0))],
        out_specs=[row(D_MODEL), row(D_MODEL)],
        out_shape=[jax.ShapeDtypeStruct((t, D_MODEL), F32),
                   jax.ShapeDtypeStruct((t, D_MODEL), BF16)],
        compiler_params=_params(("parallel",)),
        name="even_outproj",
    )(a, r, bvg, bvg, bvg, x, w, ng)


_CONV_ROWS = 64
_MROWS = 1024
_TAP_LEAD = HALO - (CONV_KERNEL - 1)


def _conv_chunk(e, w_ref, bias, cols):
    nrow = e.shape[0]
    acc = jnp.broadcast_to(bias, (_CONV_ROWS, LANES))
    for s in range(SUBLANES):
        es = e if s == 0 else pltpu.roll(e, nrow - s, 0)
        for q in range(HALO // SUBLANES + 1):
            k = SUBLANES * q + s - _TAP_LEAD
            if 0 <= k < CONV_KERNEL:
                acc = acc + w_ref[k:k + 1, cols] * es[SUBLANES * q:SUBLANES * q + _CONV_ROWS]
    return acc


def _odd_inproj_conv_kernel(h_ref, wa_ref, wb_ref, wg_ref, ba_ref, bb_ref, bg_ref, cw_ref, cb_ref,
                            c_ref, sg_ref, wbf_ref, ucur_ref, uprev_ref, *, bm, bn, nm, tiles_per_seq):
    s = pl.program_id(0)
    last = pl.num_programs(0) - 2
    i = jnp.minimum(s, last) % nm

    @pl.when(s == 0)
    def _():
        uprev_ref[...] = jnp.zeros(uprev_ref.shape, F32)

    @pl.when(i == 0)
    def _():
        wbf_ref[0] = wa_ref[...].astype(BF16)
        wbf_ref[1] = wb_ref[...].astype(BF16)
        wbf_ref[2] = wg_ref[...].astype(BF16)

    def conv_rows(rc, anchor):
        r0 = rc * _CONV_ROWS
        for lg in range(bn // LANES):
            cols = slice(lg * LANES, (lg + 1) * LANES)
            e = uprev_ref[r0:r0 + _CONV_ROWS + HALO, cols]
            acc = _conv_chunk(e, cw_ref, cb_ref[:, cols] + anchor, cols)
            c_ref[r0:r0 + _CONV_ROWS, cols] = acc.astype(c_ref.dtype)

    def zero_after(x):
        bits = pltpu.bitcast(x[0:SUBLANES, 0:LANES], jnp.uint32)
        bits = lax.shift_right_logical(lax.shift_right_logical(bits, jnp.uint32(16)), jnp.uint32(16))
        return pltpu.bitcast(bits, F32)[0:1, :]

    nrc = bm // _CONV_ROWS
    ndots = 3 * (bm // _MROWS) * (bn // _SUB)
    state = {"done": 0, "dots": 0}

    def conv_group(anchor_src, weight=1):
        state["dots"] += weight
        upto = state["dots"] * nrc // ndots
        anchor = zero_after(anchor_src)
        for rc in range(state["done"], upto):
            conv_rows(rc, anchor)
        state["done"] = upto

    for p in range((bm // _MROWS) * (bn // _SUB)):
        rb, k = divmod(p, bn // _SUB)
        rows = slice(rb * _MROWS, (rb + 1) * _MROWS)
        cols = slice(k * _SUB, (k + 1) * _SUB)
        h = h_ref[rows, :]
        a = _dot(h, wbf_ref[0, :, cols]) + ba_ref[:, cols]
        conv_group(a)
        b = _dot(h, wbf_ref[1, :, cols]) + bb_ref[:, cols]
        ucur_ref[rows, cols] = a * _sigmoid(b)
        conv_group(b)
        g = _dot(h, wbf_ref[2, :, cols]) + bg_ref[:, cols]
        sg_ref[rows, cols] = (g * _sigmoid(g)).astype(sg_ref.dtype)
        conv_group(g)

    first_in_seq = (i % tiles_per_seq) == 0
    uprev_ref[0:HALO, :] = jnp.where(first_in_seq, 0.0, uprev_ref[bm:bm + HALO, :])
    uprev_ref[HALO:HALO + bm, :] = ucur_ref[...]


def _odd_inproj_conv(h, w, b, cw, cb, seq, bm=1024, bn=512):
    t, d = h.shape
    nb = CONV_WIDTH // bn
    nm = t // bm
    last = nb * nm - 1
    prod = lambda s: jnp.minimum(s, last)
    cons = lambda s: jnp.maximum(s - 1, 0)
    wspec = lambda k: pl.BlockSpec((d, bn), lambda s: (0, prod(s) // nm + k * nb),
                                   pipeline_mode=pl.Buffered(1))
    bspec = lambda k: pl.BlockSpec((1, bn), lambda s: (0, prod(s) // nm + k * nb))
    return pl.pallas_call(
        functools.partial(_odd_inproj_conv_kernel, bm=bm, bn=bn, nm=nm, tiles_per_seq=seq // bm),
        grid=(nb * nm + 1,),
        in_specs=[pl.BlockSpec((bm, d), lambda s: (prod(s) % nm, 0)),
                  wspec(0), wspec(1), wspec(2), bspec(0), bspec(1), bspec(2),
                  pl.BlockSpec((HALO, bn), lambda s: (0, cons(s) // nm)),
                  pl.BlockSpec((1, bn), lambda s: (0, cons(s) // nm))],
        out_specs=[pl.BlockSpec((bm, bn), lambda s: (cons(s) % nm, cons(s) // nm)),
                   pl.BlockSpec((bm, bn), lambda s: (prod(s) % nm, prod(s) // nm))],
        out_shape=[jax.ShapeDtypeStruct((t, CONV_WIDTH), BF16)] * 2,
        scratch_shapes=[pltpu.VMEM((3, d, bn), BF16),
                        pltpu.VMEM((bm, bn), F32),
                        pltpu.VMEM((HALO + bm, bn), F32)],
        compiler_params=_params(("arbitrary",)),
        name="odd_inproj_conv",
    )(h, w, w, w, b, b, b, cw, cb)


def _odd_out_kernel(c_ref, sg_ref, lg_ref, lb_ref, w_ref, x_ref, bo_ref, fg_ref, o_ref):
    c = c_ref[...].astype(F32)
    mu = jnp.mean(c, axis=-1, keepdims=True)
    cen = c - mu
    var = jnp.mean(cen * cen, axis=-1, keepdims=True)
    y = cen * lax.rsqrt(var + EPS) * lg_ref[...] + lb_ref[...]
    z = (y * _sigmoid(y)) * sg_ref[...].astype(F32)
    x2 = x_ref[...] + _dot(z.astype(BF16), w_ref[...]) + bo_ref[...]
    o_ref[...] = _rms(x2, fg_ref[...])


def _odd_out(c, sg, lg, lb, w, x, bo, fg, bm=256):
    t = x.shape[0]
    row = lambda width: pl.BlockSpec((bm, width), lambda i: (i, 0))
    vec = lambda width: pl.BlockSpec((1, width), lambda i: (0, 0))
    return pl.pallas_call(
        _odd_out_kernel,
        grid=(t // bm,),
        in_specs=[row(CONV_WIDTH), row(CONV_WIDTH), vec(CONV_WIDTH), vec(CONV_WIDTH),
                  pl.BlockSpec(w.shape, lambda i: (0, 0), pipeline_mode=pl.Buffered(1)),
                  row(D_MODEL), vec(D_MODEL), vec(D_MODEL)],
        out_specs=row(D_MODEL),
        out_shape=jax.ShapeDtypeStruct((t, D_MODEL), F32),
        compiler_params=_params(("parallel",)),
        name="odd_outproj",
    )(c, sg, lg, lb, w, x, bo, fg)


def _trunk(x, norm_even, w_in_even, w_out_even, norm_odd, w_in_odd, b_in_odd, conv_w_odd,
           conv_b_odd, ln_g_odd, ln_b_odd, w_out_odd, b_out_odd, final_norm):
    batch, seq, d = x.shape
    t = batch * seq
    row = lambda v: v.reshape(1, -1)
    x2d = x.reshape(t, d)

    h, hc = _rmsnorm(x2d, row(norm_even[0]), batch, seq)
    hc = hc.reshape(t, d)
    w_in = w_in_even[0]
    tab_a, tab_b = _rope_tables(seq)
    qk_w, v_w = 6 * A_WIDTH, 3 * A_WIDTH
    bqk_w = 2 * RET_HEADS * RET_QK_DIM
    aqk = _inproj(hc, w_in, 0, qk_w, "even_inproj_aqk", "rope_a", tab_a, 3, seq)
    av = _inproj(hc, w_in, qk_w, v_w, "even_inproj_av")
    bqk = _inproj(h, w_in, qk_w + v_w, bqk_w, "even_inproj_bqk", "rope_b", tab_b, 1, seq)
    bvg = _inproj(h, w_in, qk_w + v_w + bqk_w, w_in.shape[1] - qk_w - v_w - bqk_w, "even_inproj_bvg")
    lcls = seq // NCLS
    a16 = _attention(aqk.reshape(batch, NCLS, lcls, qk_w), av.reshape(batch, NCLS, lcls, v_w),
                     _attention_masks(), batch, seq)
    a = a16.transpose(0, 2, 1, 3).reshape(t, A_WIDTH)
    r = _retention(bqk, bvg, _retention_tables(), batch, seq)
    x1, h2 = _even_out(a, r, bvg, x2d, w_out_even[0].astype(BF16), row(norm_odd[0]))

    cw = jnp.concatenate([conv_w_odd[0], jnp.zeros((HALO - CONV_KERNEL, CONV_WIDTH), F32)], axis=0)
    c, sg = _odd_inproj_conv(h2, w_in_odd[0], row(b_in_odd[0]), cw, row(conv_b_odd[0]), seq)
    out = _odd_out(c, sg, row(ln_g_odd[0]), row(ln_b_odd[0]), w_out_odd[0].astype(BF16),
                   x1, row(b_out_odd[0]), row(final_norm))
    return out.reshape(batch, seq, d)


def kernel(x, norm_even, w_in_even, w_out_even, norm_odd, w_in_odd, b_in_odd, conv_w_odd,
           conv_b_odd, ln_g_odd, ln_b_odd, w_out_odd, b_out_odd, final_norm):
    return _trunk(x, norm_even, w_in_even, w_out_even, norm_odd, w_in_odd, b_in_odd,
                  conv_w_odd, conv_b_odd, ln_g_odd, ln_b_odd, w_out_odd, b_out_odd, final_norm)
```

```python
import functools

import numpy as np
import jax
import jax.numpy as jnp
from jax import lax
from jax.experimental import pallas as pl
from jax.experimental.pallas import tpu as pltpu

F32 = jnp.float32
BF16 = jnp.bfloat16

D_MODEL = 2048
A_HEADS = 8
HEAD_DIM = 128
ATT_BLOCK = 128
ROPE_THETA = 500000.0
ROT_DIM = HEAD_DIM // 4
A_WIDTH = A_HEADS * HEAD_DIM
RET_HEADS = 8
RET_QK_DIM = 128
RET_V_DIM = 256
RET_CHUNK = 128
RET_ROPE_THETA = 10000.0
CONV_WIDTH = 4096
CONV_KERNEL = 31
EPS = 1e-6
NEG_INF = -1e30

NCLS = 16
CHUNK = NCLS * ATT_BLOCK
HALO = 32
LANES = 128
SUBLANES = 8

VMEM_LIMIT_V7X = 56 * 1024 * 1024


def _params(sem):
    return pltpu.CompilerParams(dimension_semantics=sem, vmem_limit_bytes=VMEM_LIMIT_V7X)


def _dot(a, b):
    return jnp.dot(a, b, preferred_element_type=F32)


def _dot_nt(a, b):
    return lax.dot_general(a, b, (((1,), (1,)), ((), ())), preferred_element_type=F32)


def _dot_tn(a, b):
    return lax.dot_general(a, b, (((0,), (0,)), ((), ())), preferred_element_type=F32)


def _sigmoid(x):
    return 1.0 / (1.0 + jnp.exp(-x))


def _rms(x, g):
    ms = jnp.mean(x * x, axis=-1, keepdims=True)
    return x * lax.rsqrt(ms + EPS) * g


def _rmsnorm_kernel(x_ref, g_ref, p_ref, h_ref, hc_ref, *, bl):
    h = _rms(x_ref[...], g_ref[...]).astype(h_ref.dtype)
    h_ref[...] = h
    hc = _dot(p_ref[...], h).astype(hc_ref.dtype)
    for r in range(NCLS):
        hc_ref[r] = hc[r * bl:(r + 1) * bl]


def _class_major_perm(bl, inverse=False):
    n = bl * NCLS
    out_row = np.arange(n)
    src = NCLS * (out_row % bl) + out_row // bl
    mat = (src[:, None] == np.arange(n)[None, :]).astype(np.float32)
    return jnp.asarray(mat.T if inverse else mat, dtype=BF16)


def _rmsnorm(x, g, batch, seq, bl=32):
    t, d = x.shape
    lcls = seq // NCLS
    nb = lcls // bl
    perm = _class_major_perm(bl)
    return pl.pallas_call(
        functools.partial(_rmsnorm_kernel, bl=bl),
        grid=(batch, nb),
        in_specs=[pl.BlockSpec((bl * NCLS, d), lambda b, i: (b * nb + i, 0)),
                  pl.BlockSpec((1, d), lambda b, i: (0, 0)),
                  pl.BlockSpec(perm.shape, lambda b, i: (0, 0))],
        out_specs=[pl.BlockSpec((bl * NCLS, d), lambda b, i: (b * nb + i, 0)),
                   pl.BlockSpec((None, NCLS, bl, d), lambda b, i: (b, 0, i, 0))],
        out_shape=[jax.ShapeDtypeStruct((t, d), BF16),
                   jax.ShapeDtypeStruct((batch, NCLS, lcls, d), BF16)],
        compiler_params=_params(("parallel", "parallel")),
        name="rmsnorm_even",
    )(x, g, perm)


_SUB = 256


def _inproj_kernel(h_ref, w_ref, *rest, mode, bn, q_blocks):
    if mode == "plain":
        o_ref, wb_ref = rest
    else:
        tab_ref, o_ref, wb_ref = rest
        ntab = 3 if mode == "rope_a" else 2
        base = jnp.where(pl.program_id(0) < q_blocks, 0, ntab)

    @pl.when(pl.program_id(1) == 0)
    def _():
        wb_ref[...] = w_ref[...].astype(wb_ref.dtype)

    h = h_ref[...]
    for k in range(bn // _SUB):
        acc = _dot(h, wb_ref[:, k * _SUB:(k + 1) * _SUB])
        for hh in range(_SUB // LANES):
            x = acc[:, hh * LANES:(hh + 1) * LANES]
            if mode == "rope_a":
                x = x * tab_ref[base] + pltpu.roll(x, ROT_DIM // 2, 1) * tab_ref[base + 1] \
                    + pltpu.roll(x, HEAD_DIM - ROT_DIM // 2, 1) * tab_ref[base + 2]
            elif mode == "rope_b":
                x = x * tab_ref[base] + pltpu.roll(x, RET_QK_DIM // 2, 1) * tab_ref[base + 1]
            col = k * _SUB + hh * LANES
            o_ref[:, col:col + LANES] = x.astype(o_ref.dtype)


def _inproj(h, w, col0, ncols, name, mode="plain", tabs=None, q_blocks=0, seq=None,
            bm=1024, bn=1024):
    t, d = h.shape
    cb0 = col0 // bn
    in_specs = [pl.BlockSpec((bm, d), lambda j, i: (i, 0)),
                pl.BlockSpec((d, bn), lambda j, i: (0, cb0 + j))]
    args = [h, w]
    if mode != "plain":
        sblocks = seq // bm
        in_specs.append(pl.BlockSpec((tabs.shape[0], bm, LANES), lambda j, i: (0, i % sblocks, 0)))
        args.append(tabs)
    return pl.pallas_call(
        functools.partial(_inproj_kernel, mode=mode, bn=bn, q_blocks=q_blocks),
        grid=(ncols // bn, t // bm),
        in_specs=in_specs,
        out_specs=pl.BlockSpec((bm, bn), lambda j, i: (i, j)),
        out_shape=jax.ShapeDtypeStruct((t, ncols), BF16),
        scratch_shapes=[pltpu.VMEM((d, bn), BF16)],
        compiler_params=_params(("arbitrary", "arbitrary")),
        name=name,
    )(*args)


def _rope_tables(seq):
    pos = jnp.arange(seq, dtype=F32)
    half = ROT_DIM // 2
    inv = ROPE_THETA ** (-jnp.arange(0, ROT_DIM, 2, dtype=F32) / ROT_DIM)
    ang = pos[:, None] * inv[None, :]
    cos, sin = jnp.cos(ang), jnp.sin(ang)
    zeros = jnp.zeros((seq, HEAD_DIM - ROT_DIM), F32)
    zh = jnp.zeros((seq, half), F32)
    c_a = jnp.concatenate([cos, cos, jnp.ones_like(zeros)], axis=-1)
    sa_a = jnp.concatenate([zh, sin, zeros], axis=-1)
    sb_a = jnp.concatenate([-sin, zh, zeros], axis=-1)
    qs = HEAD_DIM ** -0.5
    tab_a = jnp.stack([c_a * qs, sa_a * qs, sb_a * qs, c_a, sa_a, sb_a], axis=0)
    tab_a = tab_a.reshape(6, seq // NCLS, NCLS, LANES).transpose(0, 2, 1, 3).reshape(6, seq, LANES)
    invb = RET_ROPE_THETA ** (-jnp.arange(0, RET_QK_DIM, 2, dtype=F32) / RET_QK_DIM)
    angb = pos[:, None] * invb[None, :]
    cosb, sinb = jnp.cos(angb), jnp.sin(angb)
    c_b = jnp.concatenate([cosb, cosb], axis=-1)
    s_b = jnp.concatenate([-sinb, sinb], axis=-1)
    ks = RET_QK_DIM ** -0.5
    tab_b = jnp.stack([c_b, s_b, c_b * ks, s_b * ks], axis=0)
    return tab_a, tab_b


def _attention_masks():
    def table(pos_q, pos_k, first_from):
        dist = pos_q[:, None] - pos_k[None, :]
        ok = (dist >= 0) & (dist <= ATT_BLOCK)
        first = ok & (pos_k[None, :] >= first_from)
        return np.where(np.stack([ok, first]), 0.0, NEG_INF).astype(np.float32)

    i = np.arange(ATT_BLOCK)
    m16 = table(ATT_BLOCK + i, np.arange(2 * ATT_BLOCK), ATT_BLOCK)
    a, j = np.divmod(np.arange(ATT_BLOCK), 32)
    ak, jk = np.divmod(np.arange(2 * ATT_BLOCK), 64)
    m4 = table(ATT_BLOCK + 4 * j + a, 4 * jk + ak, ATT_BLOCK)
    r, j = np.divmod(np.arange(ATT_BLOCK), 8)
    rk, jk = np.divmod(np.arange(2 * ATT_BLOCK), 16)
    m1 = table(ATT_BLOCK + 16 * j + r, 16 * jk + rk, ATT_BLOCK)
    return jnp.asarray(m1), jnp.asarray(m4), jnp.asarray(m16)


def _online_block(q, k, v, bias, m_old, l_old, acc_old):
    nq, nk = q.shape[0], k.shape[0]
    s = _dot_nt(q, k) + bias
    m_blk = jnp.max(s, axis=1, keepdims=True)
    if m_old is None:
        m_new = jnp.broadcast_to(m_blk, (nq, LANES))
    else:
        m_new = jnp.maximum(m_old, m_blk)
    p = jnp.exp(s - jnp.concatenate([m_new] * (nk // LANES), axis=1))
    l_blk = jnp.sum(p, axis=1, keepdims=True)
    o = _dot(p.astype(BF16), v)
    if m_old is None:
        return m_new, jnp.broadcast_to(l_blk, (nq, LANES)), o
    alpha = jnp.exp(m_old - m_new)
    return m_new, alpha * l_old + l_blk, alpha * acc_old + o


_U16, _U4, _U1 = 16, 4, 16


def _attention_kernel(q1_ref, q4_ref, q16_ref, kc1_ref, kc4_ref, kc16_ref,
                      kp1_ref, kp4_ref, kp16_ref, vc1_ref, vc4_ref, vc16_ref,
                      vp1_ref, vp4_ref, vp16_ref, m1_ref, m4_ref, m16_ref,
                      o_ref, k4_ref, v4_ref, k1_ref, v1_ref, acc_ref, mst_ref, lst_ref, q1f_ref):
    first_chunk = pl.program_id(1) == 0
    cat = lambda parts: jnp.concatenate(parts, axis=0)

    def body16(i, carry):
        bias = m16_ref[first_chunk.astype(jnp.int32)]
        res = []
        for u in range(_U16):
            r = i * _U16 + u
            k = cat([kp16_ref[r], kc16_ref[r]])
            v = cat([vp16_ref[r], vc16_ref[r]])
            res.append(_online_block(q16_ref[r], k, v, bias, None, None, None))
        for u, (m, l, o) in enumerate(res):
            r = i * _U16 + u
            mst_ref[r] = m
            lst_ref[r] = l
            acc_ref[r] = o
        return carry

    lax.fori_loop(0, NCLS // _U16, body16, 0)

    k4_ref[:, 0:32, :] = kp4_ref[...]
    k4_ref[:, 32:32 + ATT_BLOCK, :] = kc4_ref[...]
    v4_ref[:, 0:32, :] = vp4_ref[...]
    v4_ref[:, 32:32 + ATT_BLOCK, :] = vc4_ref[...]

    def body4(i, carry):
        loaded = []
        for u in range(_U4):
            nn = i * _U4 + u
            qrows = pl.ds(pl.multiple_of(nn * 32, 32), 32)
            krows = pl.ds(pl.multiple_of(nn * 32, 32), 64)
            bias = m4_ref[jnp.logical_and(first_chunk, nn == 0).astype(jnp.int32)]
            for r4 in range(4):
                cls = [r4 + 4 * a for a in range(4)]
                loaded.append((cat([q4_ref[cl, qrows, :] for cl in cls]),
                               cat([k4_ref[cl, krows, :] for cl in cls]),
                               cat([v4_ref[cl, krows, :] for cl in cls]), bias,
                               cat([mst_ref[cl, qrows, :] for cl in cls]),
                               cat([lst_ref[cl, qrows, :] for cl in cls]),
                               cat([acc_ref[cl, qrows, :] for cl in cls])))
        res = [_online_block(*args) for args in loaded]
        for n, (m, l, o) in enumerate(res):
            u, r4 = divmod(n, 4)
            qrows = pl.ds(pl.multiple_of((i * _U4 + u) * 32, 32), 32)
            for a in range(4):
                cl = r4 + 4 * a
                part = slice(32 * a, 32 * (a + 1))
                mst_ref[cl, qrows, :] = m[part]
                lst_ref[cl, qrows, :] = l[part]
                acc_ref[cl, qrows, :] = o[part]
        return carry

    lax.fori_loop(0, 4 // _U4, body4, 0)

    q1f_ref[...] = q1_ref[...].astype(F32)
    k1_ref[:, 0:16, :] = kp1_ref[...].astype(F32)
    k1_ref[:, 16:16 + ATT_BLOCK, :] = kc1_ref[...].astype(F32)
    v1_ref[:, 0:16, :] = vp1_ref[...].astype(F32)
    v1_ref[:, 16:16 + ATT_BLOCK, :] = vc1_ref[...].astype(F32)

    def body1(i, carry):
        loaded = []
        for u in range(_U1):
            mb = i * _U1 + u
            qrows = pl.ds(pl.multiple_of(mb * SUBLANES, SUBLANES), SUBLANES)
            krows = pl.ds(pl.multiple_of(mb * SUBLANES + SUBLANES, SUBLANES), 2 * SUBLANES)
            bias = m1_ref[jnp.logical_and(first_chunk, mb == 0).astype(jnp.int32)]
            rng = range(NCLS)
            loaded.append((cat([q1f_ref[cl, qrows, :] for cl in rng]).astype(BF16),
                           cat([k1_ref[cl, krows, :] for cl in rng]).astype(BF16),
                           cat([v1_ref[cl, krows, :] for cl in rng]).astype(BF16), bias,
                           cat([mst_ref[cl, qrows, :] for cl in rng]),
                           cat([lst_ref[cl, qrows, :] for cl in rng]),
                           cat([acc_ref[cl, qrows, :] for cl in rng])))
        res = [_online_block(*args) for args in loaded]
        for u, (_, l, o) in enumerate(res):
            qrows = pl.ds(pl.multiple_of((i * _U1 + u) * SUBLANES, SUBLANES), SUBLANES)
            out = o / l
            for cl in range(NCLS):
                acc_ref[cl, qrows, :] = out[SUBLANES * cl:SUBLANES * (cl + 1)]
        return carry

    lax.fori_loop(0, NCLS // _U1, body1, 0)
    o_ref[...] = acc_ref[...].astype(o_ref.dtype)


def _attention(qk, v, masks, batch, seq):
    lcls = seq // NCLS
    nchunk = seq // CHUNK
    nh = A_HEADS
    m1, m4, m16 = masks

    def cur(colbase):
        return pl.BlockSpec((None, NCLS, ATT_BLOCK, HEAD_DIM),
                            lambda b, c, h: (b, 0, c, colbase + h))

    def prev(colbase, rows):
        per = ATT_BLOCK // rows
        return pl.BlockSpec((None, NCLS, rows, HEAD_DIM),
                            lambda b, c, h: (b, 0, jnp.maximum(c * per - 1, 0), colbase + h))

    def const(m):
        return pl.BlockSpec(m.shape, lambda b, c, h: (0, 0, 0))

    kb = 3 * nh
    in_specs = [cur(0), cur(nh), cur(2 * nh),
                cur(kb), cur(kb + nh), cur(kb + 2 * nh),
                prev(kb, 16), prev(kb + nh, 32), prev(kb + 2 * nh, ATT_BLOCK),
                cur(0), cur(nh), cur(2 * nh),
                prev(0, 16), prev(nh, 32), prev(2 * nh, ATT_BLOCK),
                const(m1), const(m4), const(m16)]
    return pl.pallas_call(
        _attention_kernel,
        grid=(batch, nchunk, nh),
        in_specs=in_specs,
        out_specs=pl.BlockSpec((None, NCLS, ATT_BLOCK, HEAD_DIM), lambda b, c, h: (b, 0, c, h)),
        out_shape=jax.ShapeDtypeStruct((batch, NCLS, lcls, A_WIDTH), BF16),
        scratch_shapes=[pltpu.VMEM((NCLS, 32 + ATT_BLOCK, HEAD_DIM), BF16),
                        pltpu.VMEM((NCLS, 32 + ATT_BLOCK, HEAD_DIM), BF16),
                        pltpu.VMEM((NCLS, 16 + ATT_BLOCK, HEAD_DIM), F32),
                        pltpu.VMEM((NCLS, 16 + ATT_BLOCK, HEAD_DIM), F32),
                        pltpu.VMEM((NCLS, ATT_BLOCK, HEAD_DIM), F32),
                        pltpu.VMEM((NCLS, ATT_BLOCK, HEAD_DIM), F32),
                        pltpu.VMEM((NCLS, ATT_BLOCK, HEAD_DIM), F32),
                        pltpu.VMEM((NCLS, ATT_BLOCK, HEAD_DIM), F32)],
        compiler_params=_params(("parallel", "parallel", "parallel")),
        name="dilated_attention",
    )(*([qk] * 9), *([v] * 6), m1, m4, m16)


def _retention_kernel(cd_ref, q_ref, k_ref, v0_ref, v1_ref, dm_ref, kd_ref, qd_ref,
                      o_ref, st_ref):
    @pl.when(pl.program_id(1) == 0)
    def _():
        st_ref[...] = jnp.zeros_like(st_ref)

    for h in range(RET_HEADS):
        cols = slice(h * RET_QK_DIM, (h + 1) * RET_QK_DIM)
        q = q_ref[:, cols]
        k = k_ref[:, cols]
        v_ref = v0_ref if h < RET_HEADS // 2 else v1_ref
        hv = h % (RET_HEADS // 2)
        v = v_ref[:, hv * RET_V_DIM:(hv + 1) * RET_V_DIM]
        state = st_ref[h]
        scores = _dot_nt(q, k) * dm_ref[h]
        intra = _dot(scores.astype(BF16), v)
        q_dec = (q.astype(F32) * qd_ref[h]).astype(BF16)
        cross = _dot(q_dec, state.astype(BF16))
        out = intra + cross
        ms = jnp.mean(out * out, axis=-1, keepdims=True)
        o_ref[:, h * RET_V_DIM:(h + 1) * RET_V_DIM] = (out * lax.rsqrt(ms + EPS)).astype(o_ref.dtype)
        k_dec = (k.astype(F32) * kd_ref[h]).astype(BF16)
        st_ref[h] = state * cd_ref[h] + _dot_tn(k_dec, v)


def _retention_tables():
    c = RET_CHUNK
    log_g = jnp.log1p(-jnp.power(2.0, -5.0 - jnp.arange(RET_HEADS, dtype=F32)))
    idx = jnp.arange(c, dtype=F32)
    diff = idx[:, None] - idx[None, :]
    dmask = jnp.where(diff >= 0, jnp.exp(jnp.maximum(diff, 0.0)[None] * log_g[:, None, None]), 0.0)
    kd = jnp.exp((c - 1 - idx)[None, :] * log_g[:, None])
    qd = jnp.exp((idx + 1.0)[None, :] * log_g[:, None])
    cd = jnp.exp(c * log_g)
    rep = lambda t: jnp.broadcast_to(t[:, :, None], (RET_HEADS, c, RET_QK_DIM))
    return cd, dmask, rep(kd), rep(qd)


def _retention(bqk, bvg, tables, batch, seq):
    cd, dmask, kd, qd = tables
    t = bqk.shape[0]
    nchunk = seq // RET_CHUNK
    w = RET_HEADS * RET_QK_DIM

    def rows(col):
        return pl.BlockSpec((RET_CHUNK, w), lambda b, n: (b * nchunk + n, col))

    def const(a):
        return pl.BlockSpec(a.shape, lambda b, n: (0, 0, 0))

    return pl.pallas_call(
        _retention_kernel,
        grid=(batch, nchunk),
        in_specs=[pl.BlockSpec(memory_space=pltpu.SMEM),
                  rows(0), rows(1), rows(0), rows(1),
                  const(dmask), const(kd), const(qd)],
        out_specs=pl.BlockSpec((RET_CHUNK, RET_HEADS * RET_V_DIM), lambda b, n: (b * nchunk + n, 0)),
        out_shape=jax.ShapeDtypeStruct((t, RET_HEADS * RET_V_DIM), BF16),
        scratch_shapes=[pltpu.VMEM((RET_HEADS, RET_QK_DIM, RET_V_DIM), F32)],
        compiler_params=_params(("parallel", "arbitrary")),
        name="retention",
    )(cd, bqk, bqk, bvg, bvg, dmask, kd, qd)


def _even_out_kernel(a_ref, u_ref, r_ref, g0_ref, g1_ref, g2_ref, x_ref, w_ref, ng_ref,
                     x1_ref, h2_ref):
    def gated(y, g_ref):
        g = g_ref[...].astype(F32)
        return (y.astype(F32) * (g * _sigmoid(g))).astype(BF16)

    a = _dot(u_ref[...], jnp.concatenate([a_ref[r] for r in range(NCLS)], axis=0))
    acc = _dot(gated(a, g0_ref), w_ref[0:1024, :])
    acc += _dot(gated(r_ref[:, 0:1024], g1_ref), w_ref[1024:2048, :])
    acc += _dot(gated(r_ref[:, 1024:2048], g2_ref), w_ref[2048:3072, :])
    x1 = x_ref[...] + acc
    x1_ref[...] = x1
    h2_ref[...] = _rms(x1, ng_ref[...]).astype(h2_ref.dtype)


def _even_out(a16, r, bvg, x, w, ng, seq, bm=256):
    t = x.shape[0]
    gcol = 2
    sblocks = seq // bm
    bl = bm // NCLS
    unperm = _class_major_perm(bl, inverse=True)
    row = lambda width: pl.BlockSpec((bm, width), lambda i: (i, 0))
    gate = lambda col: pl.BlockSpec((bm, 1024), lambda i: (i, col))
    return pl.pallas_call(
        _even_out_kernel,
        grid=(t // bm,),
        in_specs=[pl.BlockSpec((None, NCLS, bl, A_WIDTH), lambda i: (i // sblocks, 0, i % sblocks, 0)),
                  pl.BlockSpec(unperm.shape, lambda i: (0, 0)),
                  row(RET_HEADS * RET_V_DIM),
                  gate(gcol), gate(gcol + 1), gate(gcol + 2),
                  row(D_MODEL),
                  pl.BlockSpec(w.shape, lambda i: (0, 0), pipeline_mode=pl.Buffered(1)),
                  pl.BlockSpec((1, D_MODEL), lambda i: (0, 0))],
        out_specs=[row(D_MODEL), row(D_MODEL)],
        out_shape=[jax.ShapeDtypeStruct((t, D_MODEL), F32),
                   jax.ShapeDtypeStruct((t, D_MODEL), BF16)],
        compiler_params=_params(("parallel",)),
        name="even_outproj",
    )(a16, unperm, r, bvg, bvg, bvg, x, w, ng)


_CONV_ROWS = 64
_MROWS = 1024
_TAP_LEAD = HALO - (CONV_KERNEL - 1)


def _conv_chunk(e, w_ref, bias, cols):
    nrow = e.shape[0]
    acc = jnp.broadcast_to(bias, (_CONV_ROWS, LANES))
    for s in range(SUBLANES):
        es = e if s == 0 else pltpu.roll(e, nrow - s, 0)
        for q in range(HALO // SUBLANES + 1):
            k = SUBLANES * q + s - _TAP_LEAD
            if 0 <= k < CONV_KERNEL:
                acc = acc + w_ref[k:k + 1, cols] * es[SUBLANES * q:SUBLANES * q + _CONV_ROWS]
    return acc


def _odd_inproj_conv_kernel(h_ref, wa_ref, wb_ref, wg_ref, ba_ref, bb_ref, bg_ref, cw_ref, cb_ref,
                            c_ref, sg_ref, wbf_ref, ucur_ref, uprev_ref, *, bm, bn, nm, tiles_per_seq):
    s = pl.program_id(0)
    last = pl.num_programs(0) - 2
    i = jnp.minimum(s, last) % nm

    @pl.when(s == 0)
    def _():
        uprev_ref[...] = jnp.zeros(uprev_ref.shape, F32)

    @pl.when(i == 0)
    def _():
        wbf_ref[0] = wa_ref[...].astype(BF16)
        wbf_ref[1] = wb_ref[...].astype(BF16)
        wbf_ref[2] = wg_ref[...].astype(BF16)

    def conv_rows(rc, anchor):
        r0 = rc * _CONV_ROWS
        for lg in range(bn // LANES):
            cols = slice(lg * LANES, (lg + 1) * LANES)
            e = uprev_ref[r0:r0 + _CONV_ROWS + HALO, cols]
            acc = _conv_chunk(e, cw_ref, cb_ref[:, cols] + anchor, cols)
            c_ref[r0:r0 + _CONV_ROWS, cols] = acc.astype(c_ref.dtype)
        return acc

    def zero_after(x):
        bits = pltpu.bitcast(x[0:SUBLANES, 0:LANES], jnp.uint32)
        bits = lax.shift_right_logical(lax.shift_right_logical(bits, jnp.uint32(16)), jnp.uint32(16))
        return pltpu.bitcast(bits, F32)[0:1, :]

    nrc = bm // _CONV_ROWS
    ndots = 3 * (bm // _MROWS) * (bn // _SUB)
    state = {"done": 0, "dots": 0}

    def conv_group(anchor_src):
        state["dots"] += 1
        upto = state["dots"] * nrc // ndots
        anchor = zero_after(anchor_src)
        for rc in range(state["done"], upto):
            conv_rows(rc, anchor)
        state["done"] = upto

    for p in range((bm // _MROWS) * (bn // _SUB)):
        rb, k = divmod(p, bn // _SUB)
        rows = slice(rb * _MROWS, (rb + 1) * _MROWS)
        cols = slice(k * _SUB, (k + 1) * _SUB)
        h = h_ref[rows, :]
        a = _dot(h, wbf_ref[0, :, cols]) + ba_ref[:, cols]
        conv_group(a)
        b = _dot(h, wbf_ref[1, :, cols]) + bb_ref[:, cols]
        ucur_ref[rows, cols] = a * _sigmoid(b)
        conv_group(b)
        g = _dot(h, wbf_ref[2, :, cols]) + bg_ref[:, cols]
        sg_ref[rows, cols] = (g * _sigmoid(g)).astype(sg_ref.dtype)
        conv_group(g)

    first_in_seq = (i % tiles_per_seq) == 0
    uprev_ref[0:HALO, :] = jnp.where(first_in_seq, 0.0, uprev_ref[bm:bm + HALO, :])
    uprev_ref[HALO:HALO + bm, :] = ucur_ref[...]


def _odd_inproj_conv(h, w, b, cw, cb, seq, bm=1024, bn=512):
    t, d = h.shape
    nb = CONV_WIDTH // bn
    nm = t // bm
    last = nb * nm - 1
    prod = lambda s: jnp.minimum(s, last)
    cons = lambda s: jnp.maximum(s - 1, 0)
    wspec = lambda k: pl.BlockSpec((d, bn), lambda s: (0, prod(s) // nm + k * nb),
                                   pipeline_mode=pl.Buffered(1))
    bspec = lambda k: pl.BlockSpec((1, bn), lambda s: (0, prod(s) // nm + k * nb))
    return pl.pallas_call(
        functools.partial(_odd_inproj_conv_kernel, bm=bm, bn=bn, nm=nm, tiles_per_seq=seq // bm),
        grid=(nb * nm + 1,),
        in_specs=[pl.BlockSpec((bm, d), lambda s: (prod(s) % nm, 0)),
                  wspec(0), wspec(1), wspec(2), bspec(0), bspec(1), bspec(2),
                  pl.BlockSpec((HALO, bn), lambda s: (0, cons(s) // nm)),
                  pl.BlockSpec((1, bn), lambda s: (0, cons(s) // nm))],
        out_specs=[pl.BlockSpec((bm, bn), lambda s: (cons(s) % nm, cons(s) // nm)),
                   pl.BlockSpec((bm, bn), lambda s: (prod(s) % nm, prod(s) // nm))],
        out_shape=[jax.ShapeDtypeStruct((t, CONV_WIDTH), BF16)] * 2,
        scratch_shapes=[pltpu.VMEM((3, d, bn), BF16),
                        pltpu.VMEM((bm, bn), F32),
                        pltpu.VMEM((HALO + bm, bn), F32)],
        compiler_params=_params(("arbitrary",)),
        name="odd_inproj_conv",
    )(h, w, w, w, b, b, b, cw, cb)


def _odd_out_kernel(c_ref, sg_ref, lg_ref, lb_ref, w_ref, x_ref, bo_ref, fg_ref, o_ref):
    c = c_ref[...].astype(F32)
    mu = jnp.mean(c, axis=-1, keepdims=True)
    cen = c - mu
    var = jnp.mean(cen * cen, axis=-1, keepdims=True)
    y = cen * lax.rsqrt(var + EPS) * lg_ref[...] + lb_ref[...]
    z = (y * _sigmoid(y)) * sg_ref[...].astype(F32)
    x2 = x_ref[...] + _dot(z.astype(BF16), w_ref[...]) + bo_ref[...]
    o_ref[...] = _rms(x2, fg_ref[...])


def _odd_out(c, sg, lg, lb, w, x, bo, fg, bm=256):
    t = x.shape[0]
    row = lambda width: pl.BlockSpec((bm, width), lambda i: (i, 0))
    vec = lambda width: pl.BlockSpec((1, width), lambda i: (0, 0))
    return pl.pallas_call(
        _odd_out_kernel,
        grid=(t // bm,),
        in_specs=[row(CONV_WIDTH), row(CONV_WIDTH), vec(CONV_WIDTH), vec(CONV_WIDTH),
                  pl.BlockSpec(w.shape, lambda i: (0, 0), pipeline_mode=pl.Buffered(1)),
                  row(D_MODEL), vec(D_MODEL), vec(D_MODEL)],
        out_specs=row(D_MODEL),
        out_shape=jax.ShapeDtypeStruct((t, D_MODEL), F32),
        compiler_params=_params(("parallel",)),
        name="odd_outproj",
    )(c, sg, lg, lb, w, x, bo, fg)


def _trunk(x, norm_even, w_in_even, w_out_even, norm_odd, w_in_odd, b_in_odd, conv_w_odd,
           conv_b_odd, ln_g_odd, ln_b_odd, w_out_odd, b_out_odd, final_norm):
    batch, seq, d = x.shape
    t = batch * seq
    row = lambda v: v.reshape(1, -1)
    x2d = x.reshape(t, d)

    h, hc = _rmsnorm(x2d, row(norm_even[0]), batch, seq)
    hc = hc.reshape(t, d)
    w_in = w_in_even[0]
    tab_a, tab_b = _rope_tables(seq)
    qk_w, v_w = 6 * A_WIDTH, 3 * A_WIDTH
    bqk_w = 2 * RET_HEADS * RET_QK_DIM
    aqk = _inproj(hc, w_in, 0, qk_w, "even_inproj_aqk", "rope_a", tab_a, 3, seq)
    av = _inproj(hc, w_in, qk_w, v_w, "even_inproj_av")
    bqk = _inproj(h, w_in, qk_w + v_w, bqk_w, "even_inproj_bqk", "rope_b", tab_b, 1, seq)
    bvg = _inproj(h, w_in, qk_w + v_w + bqk_w, w_in.shape[1] - qk_w - v_w - bqk_w, "even_inproj_bvg")
    lcls = seq // NCLS
    a16 = _attention(aqk.reshape(batch, NCLS, lcls, qk_w), av.reshape(batch, NCLS, lcls, v_w),
                     _attention_masks(), batch, seq)
    r = _retention(bqk, bvg, _retention_tables(), batch, seq)
    x1, h2 = _even_out(a16, r, bvg, x2d, w_out_even[0].astype(BF16), row(norm_odd[0]), seq)

    cw = jnp.concatenate([conv_w_odd[0], jnp.zeros((HALO - CONV_KERNEL, CONV_WIDTH), F32)], axis=0)
    c, sg = _odd_inproj_conv(h2, w_in_odd[0], row(b_in_odd[0]), cw, row(conv_b_odd[0]), seq)
    out = _odd_out(c, sg, row(ln_g_odd[0]), row(ln_b_odd[0]), w_out_odd[0].astype(BF16),
                   x1, row(b_out_odd[0]), row(final_norm))
    return out.reshape(batch, seq, d)


def kernel(x, norm_even, w_in_even, w_out_even, norm_odd, w_in_odd, b_in_odd, conv_w_odd,
           conv_b_odd, ln_g_odd, ln_b_odd, w_out_odd, b_out_odd, final_norm):
    return _trunk(x, norm_even, w_in_even, w_out_even, norm_odd, w_in_odd, b_in_odd,
                  conv_w_odd, conv_b_odd, ln_g_odd, ln_b_odd, w_out_odd, b_out_odd, final_norm)
```

```python
import functools

import numpy as np
import jax
import jax.numpy as jnp
from jax import lax
from jax.experimental import pallas as pl
from jax.experimental.pallas import tpu as pltpu

F32 = jnp.float32
BF16 = jnp.bfloat16

D_MODEL = 2048
A_HEADS = 8
HEAD_DIM = 128
ATT_BLOCK = 128
ROPE_THETA = 500000.0
ROT_DIM = HEAD_DIM // 4
A_WIDTH = A_HEADS * HEAD_DIM
RET_HEADS = 8
RET_QK_DIM = 128
RET_V_DIM = 256
RET_CHUNK = 256
RET_ROPE_THETA = 10000.0
CONV_WIDTH = 4096
CONV_KERNEL = 31
EPS = 1e-6
NEG_INF = -1e30

NCLS = 16
CHUNK = NCLS * ATT_BLOCK
HALO = 32
LANES = 128
SUBLANES = 8

VMEM_LIMIT_V7X = 56 * 1024 * 1024


def _params(sem):
    return pltpu.CompilerParams(dimension_semantics=sem, vmem_limit_bytes=VMEM_LIMIT_V7X)


def _dot(a, b):
    return jnp.dot(a, b, preferred_element_type=F32)


def _dot_nt(a, b):
    return lax.dot_general(a, b, (((1,), (1,)), ((), ())), preferred_element_type=F32)


def _dot_tn(a, b):
    return lax.dot_general(a, b, (((0,), (0,)), ((), ())), preferred_element_type=F32)


def _sigmoid(x):
    return 1.0 / (1.0 + jnp.exp(-x))


def _rms(x, g):
    ms = jnp.mean(x * x, axis=-1, keepdims=True)
    return x * lax.rsqrt(ms + EPS) * g


def _rmsnorm_kernel(x_ref, g_ref, p_ref, h_ref, hc_ref, *, bl):
    h = _rms(x_ref[...], g_ref[...]).astype(h_ref.dtype)
    h_ref[...] = h
    hc = _dot(p_ref[...], h).astype(hc_ref.dtype)
    for r in range(NCLS):
        hc_ref[r] = hc[r * bl:(r + 1) * bl]


def _class_major_perm(bl, inverse=False):
    n = bl * NCLS
    out_row = np.arange(n)
    src = NCLS * (out_row % bl) + out_row // bl
    mat = (src[:, None] == np.arange(n)[None, :]).astype(np.float32)
    return jnp.asarray(mat.T if inverse else mat, dtype=BF16)


def _rmsnorm(x, g, batch, seq, bl=32):
    t, d = x.shape
    lcls = seq // NCLS
    nb = lcls // bl
    perm = _class_major_perm(bl)
    return pl.pallas_call(
        functools.partial(_rmsnorm_kernel, bl=bl),
        grid=(batch, nb),
        in_specs=[pl.BlockSpec((bl * NCLS, d), lambda b, i: (b * nb + i, 0)),
                  pl.BlockSpec((1, d), lambda b, i: (0, 0)),
                  pl.BlockSpec(perm.shape, lambda b, i: (0, 0))],
        out_specs=[pl.BlockSpec((bl * NCLS, d), lambda b, i: (b * nb + i, 0)),
                   pl.BlockSpec((None, NCLS, bl, d), lambda b, i: (b, 0, i, 0))],
        out_shape=[jax.ShapeDtypeStruct((t, d), BF16),
                   jax.ShapeDtypeStruct((batch, NCLS, lcls, d), BF16)],
        compiler_params=_params(("parallel", "parallel")),
        name="rmsnorm_even",
    )(x, g, perm)


_SUB = 256


def _inproj_kernel(h_ref, w_ref, *rest, mode, bn, q_blocks):
    if mode == "plain":
        o_ref, wb_ref = rest
    else:
        tab_ref, o_ref, wb_ref = rest
        ntab = 3 if mode == "rope_a" else 2
        base = jnp.where(pl.program_id(0) < q_blocks, 0, ntab)

    @pl.when(pl.program_id(1) == 0)
    def _():
        wb_ref[...] = w_ref[...].astype(wb_ref.dtype)

    h = h_ref[...]
    for k in range(bn // _SUB):
        acc = _dot(h, wb_ref[:, k * _SUB:(k + 1) * _SUB])
        for hh in range(_SUB // LANES):
            x = acc[:, hh * LANES:(hh + 1) * LANES]
            if mode == "rope_a":
                x = x * tab_ref[base] + pltpu.roll(x, ROT_DIM // 2, 1) * tab_ref[base + 1] \
                    + pltpu.roll(x, HEAD_DIM - ROT_DIM // 2, 1) * tab_ref[base + 2]
            elif mode == "rope_b":
                x = x * tab_ref[base] + pltpu.roll(x, RET_QK_DIM // 2, 1) * tab_ref[base + 1]
            col = k * _SUB + hh * LANES
            o_ref[:, col:col + LANES] = x.astype(o_ref.dtype)


def _inproj(h, w, col0, ncols, name, mode="plain", tabs=None, q_blocks=0, seq=None,
            bm=1024, bn=1024):
    t, d = h.shape
    cb0 = col0 // bn
    in_specs = [pl.BlockSpec((bm, d), lambda j, i: (i, 0)),
                pl.BlockSpec((d, bn), lambda j, i: (0, cb0 + j))]
    args = [h, w]
    if mode != "plain":
        sblocks = seq // bm
        in_specs.append(pl.BlockSpec((tabs.shape[0], bm, LANES), lambda j, i: (0, i % sblocks, 0)))
        args.append(tabs)
    return pl.pallas_call(
        functools.partial(_inproj_kernel, mode=mode, bn=bn, q_blocks=q_blocks),
        grid=(ncols // bn, t // bm),
        in_specs=in_specs,
        out_specs=pl.BlockSpec((bm, bn), lambda j, i: (i, j)),
        out_shape=jax.ShapeDtypeStruct((t, ncols), BF16),
        scratch_shapes=[pltpu.VMEM((d, bn), BF16)],
        compiler_params=_params(("arbitrary", "arbitrary")),
        name=name,
    )(*args)


def _rope_tables(seq):
    f32 = np.float32
    pos = np.arange(seq, dtype=f32)
    half = ROT_DIM // 2
    inv = np.power(f32(ROPE_THETA), -np.arange(0, ROT_DIM, 2, dtype=f32) / f32(ROT_DIM)).astype(f32)
    ang = pos[:, None] * inv[None, :]
    cos, sin = np.cos(ang), np.sin(ang)
    zeros = np.zeros((seq, HEAD_DIM - ROT_DIM), f32)
    zh = np.zeros((seq, half), f32)
    c_a = np.concatenate([cos, cos, np.ones_like(zeros)], axis=-1)
    sa_a = np.concatenate([zh, sin, zeros], axis=-1)
    sb_a = np.concatenate([-sin, zh, zeros], axis=-1)
    qs = f32(HEAD_DIM ** -0.5)
    tab_a = np.stack([c_a * qs, sa_a * qs, sb_a * qs, c_a, sa_a, sb_a], axis=0)
    tab_a = tab_a.reshape(6, seq // NCLS, NCLS, LANES).transpose(0, 2, 1, 3).reshape(6, seq, LANES)
    invb = np.power(f32(RET_ROPE_THETA),
                    -np.arange(0, RET_QK_DIM, 2, dtype=f32) / f32(RET_QK_DIM)).astype(f32)
    angb = pos[:, None] * invb[None, :]
    cosb, sinb = np.cos(angb), np.sin(angb)
    tab_b = np.stack([np.concatenate([cosb, cosb], axis=-1),
                      np.concatenate([-sinb, sinb], axis=-1)], axis=0)
    return jnp.asarray(tab_a.astype(f32)), jnp.asarray(tab_b.astype(f32))


def _attention_masks():
    def table(pos_q, pos_k, first_from):
        dist = pos_q[:, None] - pos_k[None, :]
        ok = (dist >= 0) & (dist <= ATT_BLOCK)
        first = ok & (pos_k[None, :] >= first_from)
        return np.where(np.stack([ok, first]), 0.0, NEG_INF).astype(np.float32)

    i = np.arange(ATT_BLOCK)
    m16 = table(ATT_BLOCK + i, np.arange(2 * ATT_BLOCK), ATT_BLOCK)
    a, j = np.divmod(np.arange(ATT_BLOCK), 32)
    ak, jk = np.divmod(np.arange(2 * ATT_BLOCK), 64)
    m4 = table(ATT_BLOCK + 4 * j + a, 4 * jk + ak, ATT_BLOCK)
    r, j = np.divmod(np.arange(ATT_BLOCK), 8)
    rk, jk = np.divmod(np.arange(2 * ATT_BLOCK), 16)
    m1 = table(ATT_BLOCK + 16 * j + r, 16 * jk + rk, ATT_BLOCK)
    return jnp.asarray(m1), jnp.asarray(m4), jnp.asarray(m16)


def _online_block(q, k, v, bias, m_old, l_old, acc_old):
    nq, nk = q.shape[0], k.shape[0]
    s = _dot_nt(q, k) + bias
    m_blk = jnp.max(s, axis=1, keepdims=True)
    if m_old is None:
        m_new = jnp.broadcast_to(m_blk, (nq, LANES))
    else:
        m_new = jnp.maximum(m_old, m_blk)
    p = jnp.exp(s - jnp.concatenate([m_new] * (nk // LANES), axis=1))
    l_blk = jnp.sum(p, axis=1, keepdims=True)
    o = _dot(p.astype(BF16), v)
    if m_old is None:
        return m_new, jnp.broadcast_to(l_blk, (nq, LANES)), o
    alpha = jnp.exp(m_old - m_new)
    return m_new, alpha * l_old + l_blk, alpha * acc_old + o


_U16, _U4, _U1 = 16, 4, 16


def _attention_kernel(q1_ref, q4_ref, q16_ref, kc1_ref, kc4_ref, kc16_ref,
                      kp1_ref, kp4_ref, kp16_ref, vc1_ref, vc4_ref, vc16_ref,
                      vp1_ref, vp4_ref, vp16_ref, m1_ref, m4_ref, m16_ref,
                      o_ref, k4_ref, v4_ref, k1_ref, v1_ref, acc_ref, mst_ref, lst_ref, q1f_ref):
    first_chunk = pl.program_id(1) == 0
    cat = lambda parts: jnp.concatenate(parts, axis=0)

    def body16(i, carry):
        bias = m16_ref[first_chunk.astype(jnp.int32)]
        res = []
        for u in range(_U16):
            r = i * _U16 + u
            k = cat([kp16_ref[r], kc16_ref[r]])
            v = cat([vp16_ref[r], vc16_ref[r]])
            res.append(_online_block(q16_ref[r], k, v, bias, None, None, None))
        for u, (m, l, o) in enumerate(res):
            r = i * _U16 + u
            mst_ref[r] = m
            lst_ref[r] = l
            acc_ref[r] = o
        return carry

    lax.fori_loop(0, NCLS // _U16, body16, 0)

    k4_ref[:, 0:32, :] = kp4_ref[...]
    k4_ref[:, 32:32 + ATT_BLOCK, :] = kc4_ref[...]
    v4_ref[:, 0:32, :] = vp4_ref[...]
    v4_ref[:, 32:32 + ATT_BLOCK, :] = vc4_ref[...]

    def body4(i, carry):
        loaded = []
        for u in range(_U4):
            nn = i * _U4 + u
            qrows = pl.ds(pl.multiple_of(nn * 32, 32), 32)
            krows = pl.ds(pl.multiple_of(nn * 32, 32), 64)
            bias = m4_ref[jnp.logical_and(first_chunk, nn == 0).astype(jnp.int32)]
            for r4 in range(4):
                cls = [r4 + 4 * a for a in range(4)]
                loaded.append((cat([q4_ref[cl, qrows, :] for cl in cls]),
                               cat([k4_ref[cl, krows, :] for cl in cls]),
                               cat([v4_ref[cl, krows, :] for cl in cls]), bias,
                               cat([mst_ref[cl, qrows, :] for cl in cls]),
                               cat([lst_ref[cl, qrows, :] for cl in cls]),
                               cat([acc_ref[cl, qrows, :] for cl in cls])))
        res = [_online_block(*args) for args in loaded]
        for n, (m, l, o) in enumerate(res):
            u, r4 = divmod(n, 4)
            qrows = pl.ds(pl.multiple_of((i * _U4 + u) * 32, 32), 32)
            for a in range(4):
                cl = r4 + 4 * a
                part = slice(32 * a, 32 * (a + 1))
                mst_ref[cl, qrows, :] = m[part]
                lst_ref[cl, qrows, :] = l[part]
                acc_ref[cl, qrows, :] = o[part]
        return carry

    lax.fori_loop(0, 4 // _U4, body4, 0)

    q1f_ref[...] = q1_ref[...].astype(F32)
    k1_ref[:, 0:16, :] = kp1_ref[...].astype(F32)
    k1_ref[:, 16:16 + ATT_BLOCK, :] = kc1_ref[...].astype(F32)
    v1_ref[:, 0:16, :] = vp1_ref[...].astype(F32)
    v1_ref[:, 16:16 + ATT_BLOCK, :] = vc1_ref[...].astype(F32)

    def body1(i, carry):
        loaded = []
        for u in range(_U1):
            mb = i * _U1 + u
            qrows = pl.ds(pl.multiple_of(mb * SUBLANES, SUBLANES), SUBLANES)
            krows = pl.ds(pl.multiple_of(mb * SUBLANES + SUBLANES, SUBLANES), 2 * SUBLANES)
            bias = m1_ref[jnp.logical_and(first_chunk, mb == 0).astype(jnp.int32)]
            rng = range(NCLS)
            loaded.append((cat([q1f_ref[cl, qrows, :] for cl in rng]).astype(BF16),
                           cat([k1_ref[cl, krows, :] for cl in rng]).astype(BF16),
                           cat([v1_ref[cl, krows, :] for cl in rng]).astype(BF16), bias,
                           cat([mst_ref[cl, qrows, :] for cl in rng]),
                           cat([lst_ref[cl, qrows, :] for cl in rng]),
                           cat([acc_ref[cl, qrows, :] for cl in rng])))
        res = [_online_block(*args) for args in loaded]
        for u, (_, l, o) in enumerate(res):
            qrows = pl.ds(pl.multiple_of((i * _U1 + u) * SUBLANES, SUBLANES), SUBLANES)
            out = o / l
            for cl in range(NCLS):
                acc_ref[cl, qrows, :] = out[SUBLANES * cl:SUBLANES * (cl + 1)]
        return carry

    lax.fori_loop(0, NCLS // _U1, body1, 0)
    o_ref[...] = acc_ref[...].astype(o_ref.dtype)


def _attention(qk, v, masks, batch, seq):
    lcls = seq // NCLS
    nchunk = seq // CHUNK
    nh = A_HEADS
    m1, m4, m16 = masks

    def cur(colbase):
        return pl.BlockSpec((None, NCLS, ATT_BLOCK, HEAD_DIM),
                            lambda b, c, h: (b, 0, c, colbase + h))

    def prev(colbase, rows):
        per = ATT_BLOCK // rows
        return pl.BlockSpec((None, NCLS, rows, HEAD_DIM),
                            lambda b, c, h: (b, 0, jnp.maximum(c * per - 1, 0), colbase + h))

    def const(m):
        return pl.BlockSpec(m.shape, lambda b, c, h: (0, 0, 0))

    kb = 3 * nh
    in_specs = [cur(0), cur(nh), cur(2 * nh),
                cur(kb), cur(kb + nh), cur(kb + 2 * nh),
                prev(kb, 16), prev(kb + nh, 32), prev(kb + 2 * nh, ATT_BLOCK),
                cur(0), cur(nh), cur(2 * nh),
                prev(0, 16), prev(nh, 32), prev(2 * nh, ATT_BLOCK),
                const(m1), const(m4), const(m16)]
    return pl.pallas_call(
        _attention_kernel,
        grid=(batch, nchunk, nh),
        in_specs=in_specs,
        out_specs=pl.BlockSpec((None, NCLS, ATT_BLOCK, HEAD_DIM), lambda b, c, h: (b, 0, c, h)),
        out_shape=jax.ShapeDtypeStruct((batch, NCLS, lcls, A_WIDTH), BF16),
        scratch_shapes=[pltpu.VMEM((NCLS, 32 + ATT_BLOCK, HEAD_DIM), BF16),
                        pltpu.VMEM((NCLS, 32 + ATT_BLOCK, HEAD_DIM), BF16),
                        pltpu.VMEM((NCLS, 16 + ATT_BLOCK, HEAD_DIM), F32),
                        pltpu.VMEM((NCLS, 16 + ATT_BLOCK, HEAD_DIM), F32),
                        pltpu.VMEM((NCLS, ATT_BLOCK, HEAD_DIM), F32),
                        pltpu.VMEM((NCLS, ATT_BLOCK, HEAD_DIM), F32),
                        pltpu.VMEM((NCLS, ATT_BLOCK, HEAD_DIM), F32),
                        pltpu.VMEM((NCLS, ATT_BLOCK, HEAD_DIM), F32)],
        compiler_params=_params(("parallel", "parallel", "parallel")),
        name="dilated_attention",
    )(*([qk] * 9), *([v] * 6), m1, m4, m16)


def _retention_kernel(cd_ref, q_ref, k_ref, v0_ref, v1_ref, dm_ref, kd_ref, qd_ref,
                      o_ref, st_ref):
    @pl.when(pl.program_id(1) == 0)
    def _():
        st_ref[...] = jnp.zeros_like(st_ref)

    for h in range(RET_HEADS):
        cols = slice(h * RET_QK_DIM, (h + 1) * RET_QK_DIM)
        q = q_ref[:, cols]
        k = k_ref[:, cols]
        v_ref = v0_ref if h < RET_HEADS // 2 else v1_ref
        hv = h % (RET_HEADS // 2)
        v = v_ref[:, hv * RET_V_DIM:(hv + 1) * RET_V_DIM]
        state = st_ref[h]
        scores = _dot_nt(q, k) * dm_ref[h]
        intra = _dot(scores.astype(BF16), v)
        q_dec = (q.astype(F32) * qd_ref[h]).astype(BF16)
        cross = _dot(q_dec, state.astype(BF16))
        out = intra + cross
        ms = jnp.mean(out * out, axis=-1, keepdims=True)
        o_ref[:, h * RET_V_DIM:(h + 1) * RET_V_DIM] = (out * lax.rsqrt(ms + EPS)).astype(o_ref.dtype)
        k_dec = (k.astype(F32) * kd_ref[h]).astype(BF16)
        st_ref[h] = state * cd_ref[h] + _dot_tn(k_dec, v)


def _retention_tables():
    f32 = np.float32
    c = RET_CHUNK
    log_g = np.log1p(-np.power(f32(2.0), f32(-5.0) - np.arange(RET_HEADS, dtype=f32))).astype(f32)
    idx = np.arange(c, dtype=f32)
    diff = idx[:, None] - idx[None, :]
    ks = f32(RET_QK_DIM ** -0.5)
    dmask = np.where(diff >= 0, np.exp(np.maximum(diff, f32(0))[None] * log_g[:, None, None]), f32(0))
    kd = np.exp((f32(c - 1) - idx)[None, :] * log_g[:, None])
    qd = np.exp((idx + f32(1))[None, :] * log_g[:, None])
    cd = np.exp(f32(c) * log_g)
    rep = lambda t: np.broadcast_to(t[:, :, None], (RET_HEADS, c, RET_QK_DIM))
    as32 = lambda t: jnp.asarray(np.ascontiguousarray(t, dtype=f32))
    return as32(cd), as32(dmask * ks), as32(rep(kd * ks)), as32(rep(qd))


def _retention(bqk, bvg, tables, batch, seq):
    cd, dmask, kd, qd = tables
    t = bqk.shape[0]
    nchunk = seq // RET_CHUNK
    w = RET_HEADS * RET_QK_DIM

    def rows(col):
        return pl.BlockSpec((RET_CHUNK, w), lambda b, n: (b * nchunk + n, col))

    def const(a):
        return pl.BlockSpec(a.shape, lambda b, n: (0, 0, 0))

    return pl.pallas_call(
        _retention_kernel,
        grid=(batch, nchunk),
        in_specs=[pl.BlockSpec(memory_space=pltpu.SMEM),
                  rows(0), rows(1), rows(0), rows(1),
                  const(dmask), const(kd), const(qd)],
        out_specs=pl.BlockSpec((RET_CHUNK, RET_HEADS * RET_V_DIM), lambda b, n: (b * nchunk + n, 0)),
        out_shape=jax.ShapeDtypeStruct((t, RET_HEADS * RET_V_DIM), BF16),
        scratch_shapes=[pltpu.VMEM((RET_HEADS, RET_QK_DIM, RET_V_DIM), F32)],
        compiler_params=_params(("parallel", "arbitrary")),
        name="retention",
    )(cd, bqk, bqk, bvg, bvg, dmask, kd, qd)


def _even_out_kernel(a_ref, u_ref, r_ref, g0_ref, g1_ref, g2_ref, x_ref, w_ref, ng_ref,
                     x1_ref, h2_ref):
    def gated(y, g_ref):
        g = g_ref[...].astype(F32)
        return (y.astype(F32) * (g * _sigmoid(g))).astype(BF16)

    a = _dot(u_ref[...], jnp.concatenate([a_ref[r] for r in range(NCLS)], axis=0))
    acc = _dot(gated(a, g0_ref), w_ref[0:1024, :])
    acc += _dot(gated(r_ref[:, 0:1024], g1_ref), w_ref[1024:2048, :])
    acc += _dot(gated(r_ref[:, 1024:2048], g2_ref), w_ref[2048:3072, :])
    x1 = x_ref[...] + acc
    x1_ref[...] = x1
    h2_ref[...] = _rms(x1, ng_ref[...]).astype(h2_ref.dtype)


def _even_out(a16, r, bvg, x, w, ng, seq, bm=256):
    t = x.shape[0]
    gcol = 2
    sblocks = seq // bm
    bl = bm // NCLS
    unperm = _class_major_perm(bl, inverse=True)
    row = lambda width: pl.BlockSpec((bm, width), lambda i: (i, 0))
    gate = lambda col: pl.BlockSpec((bm, 1024), lambda i: (i, col))
    return pl.pallas_call(
        _even_out_kernel,
        grid=(t // bm,),
        in_specs=[pl.BlockSpec((None, NCLS, bl, A_WIDTH), lambda i: (i // sblocks, 0, i % sblocks, 0)),
                  pl.BlockSpec(unperm.shape, lambda i: (0, 0)),
                  row(RET_HEADS * RET_V_DIM),
                  gate(gcol), gate(gcol + 1), gate(gcol + 2),
                  row(D_MODEL),
                  pl.BlockSpec(w.shape, lambda i: (0, 0), pipeline_mode=pl.Buffered(1)),
                  pl.BlockSpec((1, D_MODEL), lambda i: (0, 0))],
        out_specs=[row(D_MODEL), row(D_MODEL)],
        out_shape=[jax.ShapeDtypeStruct((t, D_MODEL), F32),
                   jax.ShapeDtypeStruct((t, D_MODEL), BF16)],
        compiler_params=_params(("parallel",)),
        name="even_outproj",
    )(a16, unperm, r, bvg, bvg, bvg, x, w, ng)


_CONV_ROWS = 64
_MROWS = 1024
_TAP_LEAD = HALO - (CONV_KERNEL - 1)


def _conv_chunk(e, w_ref, bias, cols):
    nrow = e.shape[0]
    acc = jnp.broadcast_to(bias, (_CONV_ROWS, LANES))
    for s in range(SUBLANES):
        es = e if s == 0 else pltpu.roll(e, nrow - s, 0)
        for q in range(HALO // SUBLANES + 1):
            k = SUBLANES * q + s - _TAP_LEAD
            if 0 <= k < CONV_KERNEL:
                acc = acc + w_ref[k:k + 1, cols] * es[SUBLANES * q:SUBLANES * q + _CONV_ROWS]
    return acc


def _odd_inproj_conv_kernel(h_ref, wa_ref, wb_ref, wg_ref, ba_ref, bb_ref, bg_ref, cw_ref, cb_ref,
                            c_ref, sg_ref, wbf_ref, ucur_ref, uprev_ref, *, bm, bn, nm, tiles_per_seq):
    s = pl.program_id(0)
    last = pl.num_programs(0) - 2
    i = jnp.minimum(s, last) % nm

    @pl.when(s == 0)
    def _():
        uprev_ref[...] = jnp.zeros(uprev_ref.shape, F32)

    @pl.when(i == 0)
    def _():
        wbf_ref[0] = wa_ref[...].astype(BF16)
        wbf_ref[1] = wb_ref[...].astype(BF16)
        wbf_ref[2] = wg_ref[...].astype(BF16)

    def conv_rows(rc, anchor):
        r0 = rc * _CONV_ROWS
        for lg in range(bn // LANES):
            cols = slice(lg * LANES, (lg + 1) * LANES)
            e = uprev_ref[r0:r0 + _CONV_ROWS + HALO, cols]
            acc = _conv_chunk(e, cw_ref, cb_ref[:, cols] + anchor, cols)
            c_ref[r0:r0 + _CONV_ROWS, cols] = acc.astype(c_ref.dtype)
        return acc

    def zero_after(x):
        bits = pltpu.bitcast(x[0:SUBLANES, 0:LANES], jnp.uint32)
        bits = lax.shift_right_logical(lax.shift_right_logical(bits, jnp.uint32(16)), jnp.uint32(16))
        return pltpu.bitcast(bits, F32)[0:1, :]

    nrc = bm // _CONV_ROWS
    ndots = 3 * (bm // _MROWS) * (bn // _SUB)
    state = {"done": 0, "dots": 0}

    def conv_group(anchor_src):
        state["dots"] += 1
        upto = state["dots"] * nrc // ndots
        anchor = zero_after(anchor_src)
        for rc in range(state["done"], upto):
            conv_rows(rc, anchor)
        state["done"] = upto

    for p in range((bm // _MROWS) * (bn // _SUB)):
        rb, k = divmod(p, bn // _SUB)
        rows = slice(rb * _MROWS, (rb + 1) * _MROWS)
        cols = slice(k * _SUB, (k + 1) * _SUB)
        h = h_ref[rows, :]
        a = _dot(h, wbf_ref[0, :, cols]) + ba_ref[:, cols]
        conv_group(a)
        b = _dot(h, wbf_ref[1, :, cols]) + bb_ref[:, cols]
        ucur_ref[rows, cols] = a * _sigmoid(b)
        conv_group(b)
        g = _dot(h, wbf_ref[2, :, cols]) + bg_ref[:, cols]
        sg_ref[rows, cols] = (g * _sigmoid(g)).astype(sg_ref.dtype)
        conv_group(g)

    first_in_seq = (i % tiles_per_seq) == 0
    uprev_ref[0:HALO, :] = jnp.where(first_in_seq, 0.0, uprev_ref[bm:bm + HALO, :])
    uprev_ref[HALO:HALO + bm, :] = ucur_ref[...]


def _odd_inproj_conv(h, w, b, cw, cb, seq, bm=1024, bn=512):
    t, d = h.shape
    nb = CONV_WIDTH // bn
    nm = t // bm
    last = nb * nm - 1
    prod = lambda s: jnp.minimum(s, last)
    cons = lambda s: jnp.maximum(s - 1, 0)
    wspec = lambda k: pl.BlockSpec((d, bn), lambda s: (0, prod(s) // nm + k * nb),
                                   pipeline_mode=pl.Buffered(1))
    bspec = lambda k: pl.BlockSpec((1, bn), lambda s: (0, prod(s) // nm + k * nb))
    return pl.pallas_call(
        functools.partial(_odd_inproj_conv_kernel, bm=bm, bn=bn, nm=nm, tiles_per_seq=seq // bm),
        grid=(nb * nm + 1,),
        in_specs=[pl.BlockSpec((bm, d), lambda s: (prod(s) % nm, 0)),
                  wspec(0), wspec(1), wspec(2), bspec(0), bspec(1), bspec(2),
                  pl.BlockSpec((cw.shape[0], bn), lambda s: (0, cons(s) // nm)),
                  pl.BlockSpec((1, bn), lambda s: (0, cons(s) // nm))],
        out_specs=[pl.BlockSpec((bm, bn), lambda s: (cons(s) % nm, cons(s) // nm)),
                   pl.BlockSpec((bm, bn), lambda s: (prod(s) % nm, prod(s) // nm))],
        out_shape=[jax.ShapeDtypeStruct((t, CONV_WIDTH), BF16)] * 2,
        scratch_shapes=[pltpu.VMEM((3, d, bn), BF16),
                        pltpu.VMEM((bm, bn), F32),
                        pltpu.VMEM((HALO + bm, bn), F32)],
        compiler_params=_params(("arbitrary",)),
        name="odd_inproj_conv",
    )(h, w, w, w, b, b, b, cw, cb)


def _odd_out_kernel(c_ref, sg_ref, lg_ref, lb_ref, w_ref, x_ref, bo_ref, fg_ref, o_ref):
    c = c_ref[...].astype(F32)
    mu = jnp.mean(c, axis=-1, keepdims=True)
    cen = c - mu
    var = jnp.mean(cen * cen, axis=-1, keepdims=True)
    y = cen * lax.rsqrt(var + EPS) * lg_ref[...] + lb_ref[...]
    z = (y * _sigmoid(y)) * sg_ref[...].astype(F32)
    x2 = x_ref[...] + _dot(z.astype(BF16), w_ref[...]) + bo_ref[...]
    o_ref[...] = _rms(x2, fg_ref[...])


def _odd_out(c, sg, lg, lb, w, x, bo, fg, bm=256):
    t = x.shape[0]
    row = lambda width: pl.BlockSpec((bm, width), lambda i: (i, 0))
    vec = lambda width: pl.BlockSpec((1, width), lambda i: (0, 0))
    return pl.pallas_call(
        _odd_out_kernel,
        grid=(t // bm,),
        in_specs=[row(CONV_WIDTH), row(CONV_WIDTH), vec(CONV_WIDTH), vec(CONV_WIDTH),
                  pl.BlockSpec(w.shape, lambda i: (0, 0), pipeline_mode=pl.Buffered(1)),
                  row(D_MODEL), vec(D_MODEL), vec(D_MODEL)],
        out_specs=row(D_MODEL),
        out_shape=jax.ShapeDtypeStruct((t, D_MODEL), F32),
        compiler_params=_params(("parallel",)),
        name="odd_outproj",
    )(c, sg, lg, lb, w, x, bo, fg)


def _trunk(x, norm_even, w_in_even, w_out_even, norm_odd, w_in_odd, b_in_odd, conv_w_odd,
           conv_b_odd, ln_g_odd, ln_b_odd, w_out_odd, b_out_odd, final_norm):
    batch, seq, d = x.shape
    t = batch * seq
    row = lambda v: v.reshape(1, -1)
    x2d = x.reshape(t, d)

    h, hc = _rmsnorm(x2d, row(norm_even[0]), batch, seq)
    hc = hc.reshape(t, d)
    w_in = w_in_even[0]
    tab_a, tab_b = _rope_tables(seq)
    qk_w, v_w = 6 * A_WIDTH, 3 * A_WIDTH
    bqk_w = 2 * RET_HEADS * RET_QK_DIM
    aqk = _inproj(hc, w_in, 0, qk_w, "even_inproj_aqk", "rope_a", tab_a, 3, seq)
    av = _inproj(hc, w_in, qk_w, v_w, "even_inproj_av")
    bqk = _inproj(h, w_in, qk_w + v_w, bqk_w, "even_inproj_bqk", "rope_b", tab_b, 2, seq)
    bvg = _inproj(h, w_in, qk_w + v_w + bqk_w, w_in.shape[1] - qk_w - v_w - bqk_w, "even_inproj_bvg")
    lcls = seq // NCLS
    a16 = _attention(aqk.reshape(batch, NCLS, lcls, qk_w), av.reshape(batch, NCLS, lcls, v_w),
                     _attention_masks(), batch, seq)
    r = _retention(bqk, bvg, _retention_tables(), batch, seq)
    x1, h2 = _even_out(a16, r, bvg, x2d, w_out_even[0].astype(BF16), row(norm_odd[0]), seq)

    c, sg = _odd_inproj_conv(h2, w_in_odd[0], row(b_in_odd[0]), conv_w_odd[0], row(conv_b_odd[0]), seq)
    out = _odd_out(c, sg, row(ln_g_odd[0]), row(ln_b_odd[0]), w_out_odd[0].astype(BF16),
                   x1, row(b_out_odd[0]), row(final_norm))
    return out.reshape(batch, seq, d)


def kernel(x, norm_even, w_in_even, w_out_even, norm_odd, w_in_odd, b_in_odd, conv_w_odd,
           conv_b_odd, ln_g_odd, ln_b_odd, w_out_odd, b_out_odd, final_norm):
    return _trunk(x, norm_even, w_in_even, w_out_even, norm_odd, w_in_odd, b_in_odd,
                  conv_w_odd, conv_b_odd, ln_g_odd, ln_b_odd, w_out_odd, b_out_odd, final_norm)
```
